```python
import jax, jax.numpy as jnp
from jax import lax
import numpy as np

D_MODEL = 2048
BATCH = 1
SEQ = 16384
DEPTH = 1
DEC_BATCH = 32
DEC_SEQ = 1
PAST_LEN = 16384
PAGE_SIZE = 128

D_MIX = D_MODEL
CONV_CH = D_MIX // 2
ATTN_WIDTH = D_MIX - CONV_CH
HEAD_DIM = 128
N_HEADS = ATTN_WIDTH // HEAD_DIM
CONV_WIDTH = 31
FFN_WIDTH = 3
D_FF = 5632
Q_BLOCK = 128
N_IN = 2 * CONV_CH + 3 * ATTN_WIDTH + N_HEADS
FORGET_BIAS = 3.0
EPS = 1e-6

kernel_name = 'hymba_conformer_fox_convffn_step'


def rmsnorm(x, g):
    xf = x.astype(jnp.float32)
    y = xf * lax.rsqrt(jnp.mean(xf * xf, axis=-1, keepdims=True) + EPS)
    return (y * g.astype(jnp.float32)).astype(x.dtype)


def layernorm(x, g, b):
    xf = x.astype(jnp.float32)
    mu = jnp.mean(xf, axis=-1, keepdims=True)
    var = jnp.mean(jnp.square(xf - mu), axis=-1, keepdims=True)
    y = (xf - mu) * lax.rsqrt(var + EPS)
    return (y * g.astype(jnp.float32) + b.astype(jnp.float32)).astype(x.dtype)


def causal_dwconv(x, state, w, b):
    c = x.shape[-1]
    k = w.shape[0]
    xp = jnp.concatenate([state.astype(x.dtype), x], axis=1)
    y = lax.conv_general_dilated(xp, w[:, None, :].astype(x.dtype), window_strides=(1,),
                                 padding='VALID', dimension_numbers=('NWC', 'WIO', 'NWC'),
                                 feature_group_count=c)
    return y + b.astype(x.dtype), xp[:, xp.shape[1] - (k - 1):]


def fox_prompt(q, k, v, logf):
    b_, s_, h_, dh = q.shape
    nb = s_ // Q_BLOCK
    c_k = jnp.transpose(jnp.cumsum(logf, axis=1), (0, 2, 1))
    qb = jnp.moveaxis(q.reshape(b_, nb, Q_BLOCK, h_, dh), 1, 0)
    cb = jnp.moveaxis(c_k.reshape(b_, h_, nb, Q_BLOCK), 2, 0)
    pos = jnp.arange(s_, dtype=jnp.int32)
    pb = pos.reshape(nb, Q_BLOCK)
    scale = HEAD_DIM ** -0.5

    def one_block(args):
        q_i, c_i, p_i = args
        s = jnp.einsum('bqhd,bkhd->bhqk', q_i, k, preferred_element_type=jnp.float32) * scale
        s = s + c_i[..., None] - c_k[:, :, None, :]
        s = jnp.where(p_i[:, None] >= pos[None, :], s, -jnp.inf)
        p = jax.nn.softmax(s, axis=-1)
        return jnp.einsum('bhqk,bkhd->bqhd', p.astype(v.dtype), v)

    o = lax.map(one_block, (qb, cb, pb))
    return jnp.moveaxis(o, 0, 1).reshape(b_, s_, h_, dh)


def fox_sample(q, k, v, logf, cache_k, cache_v, cache_logf, page_table):
    db, t_, h_, dh = q.shape
    past = page_table.shape[1] * PAGE_SIZE
    pk = cache_k[page_table].reshape(db, past, h_, dh)
    pv = cache_v[page_table].reshape(db, past, h_, dh)
    plf = cache_logf[page_table].reshape(db, past, h_).astype(jnp.float32)
    c_all = jnp.cumsum(jnp.concatenate([plf, logf], axis=1), axis=1)
    c_all = jnp.transpose(c_all, (0, 2, 1))
    c_q = c_all[:, :, past:]
    scale = HEAD_DIM ** -0.5
    s_past = jnp.einsum('bqhd,bkhd->bhqk', q, pk, preferred_element_type=jnp.float32)
    s_new = jnp.einsum('bqhd,bkhd->bhqk', q, k, preferred_element_type=jnp.float32)
    s = jnp.concatenate([s_past, s_new], axis=-1) * scale + c_q[..., None] - c_all[:, :, None, :]
    mask = jnp.concatenate([jnp.ones((t_, past), dtype=bool),
                            jnp.tril(jnp.ones((t_, t_), dtype=bool))], axis=1)
    s = jnp.where(mask, s, -jnp.inf)
    p = jax.nn.softmax(s, axis=-1).astype(v.dtype)
    return (jnp.einsum('bhqk,bkhd->bqhd', p[..., :past], pv)
            + jnp.einsum('bhqk,bkhd->bqhd', p[..., past:], v))


def decoder_layer(x, conv_state, ffn_state, attend, norm_mix_g, w_in, b_f, conv_dw_w, conv_dw_b,
                  conv_ln_g, conv_ln_b, w_out, norm_ffn_g, w_up, ffn_dw_w, ffn_dw_b, w_down):
    b_, t_, _ = x.shape
    xn = rmsnorm(x, norm_mix_g)
    z = xn @ w_in
    off = 2 * CONV_CH
    conv_a = z[..., :CONV_CH]
    conv_g = z[..., CONV_CH:off]
    q = z[..., off:off + ATTN_WIDTH].reshape(b_, t_, N_HEADS, HEAD_DIM)
    k = z[..., off + ATTN_WIDTH:off + 2 * ATTN_WIDTH].reshape(b_, t_, N_HEADS, HEAD_DIM)
    v = z[..., off + 2 * ATTN_WIDTH:off + 3 * ATTN_WIDTH].reshape(b_, t_, N_HEADS, HEAD_DIM)
    logf = jax.nn.log_sigmoid(z[..., off + 3 * ATTN_WIDTH:].astype(jnp.float32)
                              + b_f.astype(jnp.float32))
    u = conv_a * jax.nn.sigmoid(conv_g)
    u_c, new_conv = causal_dwconv(u, conv_state, conv_dw_w, conv_dw_b)
    conv_out = jax.nn.silu(layernorm(u_c, conv_ln_g, conv_ln_b))
    attn_out = attend(q, k, v, logf).reshape(b_, t_, ATTN_WIDTH)
    h = x + jnp.concatenate([conv_out, attn_out], axis=-1) @ w_out
    hn = rmsnorm(h, norm_ffn_g)
    up, new_ffn = causal_dwconv(hn @ w_up, ffn_state, ffn_dw_w, ffn_dw_b)
    y = h + (jax.nn.silu(up[..., :D_FF]) * up[..., D_FF:]) @ w_down
    return y, k, v, logf, new_conv, new_ffn


def setup_inputs(seed: int = 0) -> dict:
    key = jax.random.key(seed)
    ks = jax.random.split(key, 24)
    f32 = jnp.float32
    n_pages = PAST_LEN // PAGE_SIZE
    n_used = DEC_BATCH * n_pages
    n_pool = n_used + (n_used + 3) // 4
    nrm = lambda k, shape: jax.random.normal(k, shape, dtype=f32)
    x_prompt = nrm(ks[0], (BATCH, SEQ, D_MODEL))
    x_sample = nrm(ks[1], (DEC_BATCH, DEC_SEQ, D_MODEL))
    cache_k = nrm(ks[2], (DEPTH, n_pool, PAGE_SIZE, N_HEADS, HEAD_DIM))
    cache_v = nrm(ks[3], (DEPTH, n_pool, PAGE_SIZE, N_HEADS, HEAD_DIM))
    cache_logf = jax.nn.log_sigmoid(FORGET_BIAS + nrm(ks[4], (DEPTH, n_pool, PAGE_SIZE, N_HEADS)))
    state_conv = 0.5 * nrm(ks[5], (DEPTH, DEC_BATCH, CONV_WIDTH - 1, CONV_CH))
    state_ffn = nrm(ks[6], (DEPTH, DEC_BATCH, FFN_WIDTH - 1, 2 * D_FF))
    page_table = jax.random.permutation(ks[7], n_pool)[:n_used].reshape(DEC_BATCH, n_pages).astype(jnp.int32)
    return {
        'x_prompt': x_prompt,
        'x_sample': x_sample,
        'cache_k': cache_k,
        'cache_v': cache_v,
        'cache_logf': cache_logf,
        'state_conv': state_conv,
        'state_ffn': state_ffn,
        'page_table': page_table,
        'norm_mix_g': 1.0 + 0.02 * nrm(ks[8], (DEPTH, D_MODEL)),
        'w_in': nrm(ks[9], (DEPTH, D_MODEL, N_IN)) * D_MODEL ** -0.5,
        'b_f': FORGET_BIAS + 0.1 * nrm(ks[10], (DEPTH, N_HEADS)),
        'conv_dw_w': nrm(ks[11], (DEPTH, CONV_WIDTH, CONV_CH)) * CONV_WIDTH ** -0.5,
        'conv_dw_b': 0.02 * nrm(ks[12], (DEPTH, CONV_CH)),
        'conv_ln_g': 1.0 + 0.02 * nrm(ks[13], (DEPTH, CONV_CH)),
        'conv_ln_b': 0.02 * nrm(ks[14], (DEPTH, CONV_CH)),
        'w_out': nrm(ks[15], (DEPTH, D_MIX, D_MODEL)) * D_MIX ** -0.5,
        'norm_ffn_g': 1.0 + 0.02 * nrm(ks[16], (DEPTH, D_MODEL)),
        'w_up': nrm(ks[17], (DEPTH, D_MODEL, 2 * D_FF)) * D_MODEL ** -0.5,
        'ffn_dw_w': nrm(ks[18], (DEPTH, FFN_WIDTH, 2 * D_FF)) * FFN_WIDTH ** -0.5,
        'ffn_dw_b': 0.02 * nrm(ks[19], (DEPTH, 2 * D_FF)),
        'w_down': nrm(ks[20], (DEPTH, D_FF, D_MODEL)) * D_FF ** -0.5,
        'norm_final_g': 1.0 + 0.02 * nrm(ks[21], (D_MODEL,)),
    }


def reference(x_prompt, x_sample, cache_k, cache_v, cache_logf, state_conv, state_ffn, page_table,
              norm_mix_g, w_in, b_f, conv_dw_w, conv_dw_b, conv_ln_g, conv_ln_b, w_out,
              norm_ffn_g, w_up, ffn_dw_w, ffn_dw_b, w_down, norm_final_g):
    xp, xs = x_prompt, x_sample
    kp_l, vp_l, lp_l, cp_l, fp_l = [], [], [], [], []
    ks_l, vs_l, ls_l, cs_l, fs_l = [], [], [], [], []
    for l in range(DEPTH):
        weights = (norm_mix_g[l], w_in[l], b_f[l], conv_dw_w[l], conv_dw_b[l], conv_ln_g[l],
                   conv_ln_b[l], w_out[l], norm_ffn_g[l], w_up[l], ffn_dw_w[l], ffn_dw_b[l], w_down[l])
        conv0 = jnp.zeros((xp.shape[0], CONV_WIDTH - 1, CONV_CH), dtype=xp.dtype)
        ffn0 = jnp.zeros((xp.shape[0], FFN_WIDTH - 1, 2 * D_FF), dtype=xp.dtype)
        xp, kp, vp, lp, cp, fp = decoder_layer(xp, conv0, ffn0, fox_prompt, *weights)
        ck, cv, cl = cache_k[l], cache_v[l], cache_logf[l]

        def attend_sample(q, k, v, logf, ck=ck, cv=cv, cl=cl):
            return fox_sample(q, k, v, logf, ck, cv, cl, page_table)

        xs, ksm, vsm, lsm, csm, fsm = decoder_layer(xs, state_conv[l], state_ffn[l], attend_sample, *weights)
        kp_l.append(kp); vp_l.append(vp); lp_l.append(lp); cp_l.append(cp); fp_l.append(fp)
        ks_l.append(ksm); vs_l.append(vsm); ls_l.append(lsm); cs_l.append(csm); fs_l.append(fsm)
    y_prompt = rmsnorm(xp, norm_final_g)
    y_sample = rmsnorm(xs, norm_final_g)
    k_prompt = jnp.stack(kp_l)
    v_prompt = jnp.stack(vp_l)
    logf_prompt = jnp.stack(lp_l)
    conv_prompt = jnp.stack(cp_l)
    ffn_prompt = jnp.stack(fp_l)
    k_sample = jnp.stack(ks_l)
    v_sample = jnp.stack(vs_l)
    logf_sample = jnp.stack(ls_l)
    conv_sample = jnp.stack(cs_l)
    ffn_sample = jnp.stack(fs_l)
    return (y_prompt, y_sample, k_prompt, v_prompt, logf_prompt, conv_prompt, ffn_prompt,
            k_sample, v_sample, logf_sample, conv_sample, ffn_sample)
```

```python
import functools

import jax
import jax.numpy as jnp
from jax import lax
from jax.experimental import pallas as pl
from jax.experimental.pallas import tpu as pltpu

EPS = 1e-6
LANE = 128
SUBLANE = 8
HEAD_DIM = 128
N_HEADS = 8
ATTN_WIDTH = N_HEADS * HEAD_DIM
HEAD_ROWS = 16
SCAN_CHUNK = 256
PAGES_PER_STEP = 8
VMEM_LIMIT = 56 * 1024 * 1024

_NT = (((1,), (1,)), ((), ()))


def _bf16(x):
    return x.astype(jnp.bfloat16)


def _split3(x):
    hi = _bf16(x)
    r1 = x - hi.astype(jnp.float32)
    mid = _bf16(r1)
    lo = _bf16(r1 - mid.astype(jnp.float32))
    return hi, mid, lo


def _log_sigmoid(x):
    return jnp.minimum(x, 0.0) - jnp.log1p(jnp.exp(-jnp.abs(x)))


def _sigmoid(x):
    return 1.0 / (1.0 + jnp.exp(-x))


def _params(sem):
    return pltpu.CompilerParams(dimension_semantics=sem, vmem_limit_bytes=VMEM_LIMIT)


def _inproj_kernel(x_ref, g_ref, w_ref, wf_ref, bf_ref, tri_ref,
                   ag_ref, kv_ref, qkv_ref, lft_ref, ct_ref,
                   xn_scr, carry_scr, *, tm, with_scan, q_scale):
    i = pl.program_id(0)
    j = pl.program_id(1)

    @pl.when(j == 0)
    def _():
        x = x_ref[...]
        ms = jnp.mean(x * x, axis=-1, keepdims=True)
        xn = x * lax.rsqrt(ms + EPS) * g_ref[...]
        xn_scr[...] = _bf16(xn)
        zf = lax.dot_general(wf_ref[...], xn_scr[...], _NT,
                             preferred_element_type=jnp.float32)
        lf = _log_sigmoid(zf + bf_ref[...])
        lft_ref[...] = lf
        if with_scan:
            @pl.when(i == 0)
            def _():
                carry_scr[...] = jnp.zeros_like(carry_scr)
            carry = carry_scr[:, 0:1]
            tri = tri_ref[...]
            for c in range(tm // SCAN_CHUNK):
                sl = slice(c * SCAN_CHUNK, (c + 1) * SCAN_CHUNK)
                hi, mid, lo = _split3(lf[:, sl])
                cs = (jnp.dot(hi, tri, preferred_element_type=jnp.float32)
                      + jnp.dot(mid, tri, preferred_element_type=jnp.float32)
                      + jnp.dot(lo, tri, preferred_element_type=jnp.float32)) + carry
                ct_ref[:, sl] = cs
                carry = cs[:, SCAN_CHUNK - 1:SCAN_CHUNK]
            carry_scr[...] = jnp.broadcast_to(carry, carry_scr.shape)
        else:
            ct_ref[...] = lf

    z = jnp.dot(xn_scr[...], w_ref[...], preferred_element_type=jnp.float32)

    @pl.when(j < 2)
    def _():
        ag_ref[...] = z

    @pl.when(j == 2)
    def _():
        qkv_ref[...] = _bf16(z * q_scale)

    @pl.when(j > 2)
    def _():
        kv_ref[...] = z
        qkv_ref[...] = _bf16(z)


def _inproj(x, g, w_main, wf_t, bf_col, tri, *, tm, with_scan):
    m, d = x.shape
    tn = ATTN_WIDTH
    assert w_main.shape == (d, 5 * tn) and m % tm == 0
    kern = functools.partial(_inproj_kernel, tm=tm, with_scan=with_scan,
                             q_scale=HEAD_DIM ** -0.5)
    return pl.pallas_call(
        kern,
        grid=(m // tm, 5),
        in_specs=[
            pl.BlockSpec((tm, d), lambda i, j: (i, 0)),
            pl.BlockSpec((1, d), lambda i, j: (0, 0)),
            pl.BlockSpec((d, tn), lambda i, j: (0, j)),
            pl.BlockSpec((HEAD_ROWS, d), lambda i, j: (0, 0)),
            pl.BlockSpec((HEAD_ROWS, 1), lambda i, j: (0, 0)),
            pl.BlockSpec((SCAN_CHUNK, SCAN_CHUNK), lambda i, j: (0, 0)),
        ],
        out_specs=[
            pl.BlockSpec((tm, tn), lambda i, j: (i, jnp.minimum(j, 1))),
            pl.BlockSpec((tm, tn), lambda i, j: (i, jnp.clip(j - 3, 0, 1))),
            pl.BlockSpec((tm, tn), lambda i, j: (i, jnp.clip(j - 2, 0, 2))),
            pl.BlockSpec((HEAD_ROWS, tm), lambda i, j: (0, i)),
            pl.BlockSpec((HEAD_ROWS, tm), lambda i, j: (0, i)),
        ],
        out_shape=[
            jax.ShapeDtypeStruct((m, 2 * tn), jnp.float32),
            jax.ShapeDtypeStruct((m, 2 * tn), jnp.float32),
            jax.ShapeDtypeStruct((m, 3 * tn), jnp.bfloat16),
            jax.ShapeDtypeStruct((HEAD_ROWS, m), jnp.float32),
            jax.ShapeDtypeStruct((HEAD_ROWS, m), jnp.float32),
        ],
        scratch_shapes=[pltpu.VMEM((tm, d), jnp.bfloat16),
                        pltpu.VMEM((HEAD_ROWS, LANE), jnp.float32)],
        compiler_params=_params(("arbitrary", "arbitrary")),
        name="inproj",
    )(x, g, w_main, wf_t, bf_col, tri)


def _layernorm_swish(y, g, b):
    mu = jnp.mean(y, axis=-1, keepdims=True)
    yc = y - mu
    var = jnp.mean(yc * yc, axis=-1, keepdims=True)
    yn = yc * lax.rsqrt(var + EPS) * g + b
    return yn * _sigmoid(yn)


def _conv_prompt_kernel(a_ref, gate_ref, w_ref, b_ref, lng_ref, lnb_ref,
                        out_ref, tail_ref, uext_scr, y_scr, *, tb, kw, hist):
    i = pl.program_id(0)

    @pl.when(i == 0)
    def _():
        uext_scr[0:hist, :] = jnp.zeros((hist, uext_scr.shape[1]), jnp.float32)

    uext_scr[hist:hist + tb, :] = a_ref[...] * _sigmoid(gate_ref[...])
    ch = a_ref.shape[1]
    rows = 128
    off = hist - (kw - 1)
    for cc in range(ch // LANE):
        cs = slice(cc * LANE, (cc + 1) * LANE)
        for rc in range(tb // rows):
            acc = jnp.broadcast_to(b_ref[:, cs], (rows, LANE))
            for t in range(kw):
                r0 = rc * rows + off + t
                acc = acc + w_ref[t:t + 1, cs] * uext_scr[r0:r0 + rows, cs]
            y_scr[rc * rows:(rc + 1) * rows, cs] = acc
    out_ref[...] = _bf16(_layernorm_swish(y_scr[...], lng_ref[...], lnb_ref[...]))
    tail = uext_scr[tb:tb + hist, :]
    uext_scr[0:hist, :] = tail
    tail_ref[...] = tail


def _conv_prompt(ag, w, b, lng, lnb, *, tb):
    s = ag.shape[0]
    ch = ag.shape[1] // 2
    kw = w.shape[0]
    hist = 32
    assert kw - 1 <= hist and s % tb == 0
    kern = functools.partial(_conv_prompt_kernel, tb=tb, kw=kw, hist=hist)
    return pl.pallas_call(
        kern,
        grid=(s // tb,),
        in_specs=[
            pl.BlockSpec((tb, ch), lambda i: (i, 0)),
            pl.BlockSpec((tb, ch), lambda i: (i, 1)),
            pl.BlockSpec((kw, ch), lambda i: (0, 0)),
            pl.BlockSpec((1, ch), lambda i: (0, 0)),
            pl.BlockSpec((1, ch), lambda i: (0, 0)),
            pl.BlockSpec((1, ch), lambda i: (0, 0)),
        ],
        out_specs=[
            pl.BlockSpec((tb, ch), lambda i: (i, 0)),
            pl.BlockSpec((hist, ch), lambda i: (0, 0)),
        ],
        out_shape=[
            jax.ShapeDtypeStruct((s, ch), jnp.bfloat16),
            jax.ShapeDtypeStruct((hist, ch), jnp.float32),
        ],
        scratch_shapes=[pltpu.VMEM((hist + tb, ch), jnp.float32),
                        pltpu.VMEM((tb, ch), jnp.float32)],
        compiler_params=_params(("arbitrary",)),
        name="conv_prompt",
    )(ag, ag, w, b, lng, lnb)


def _conv_sample_kernel(a_ref, gate_ref, st_ref, w_ref, b_ref, lng_ref, lnb_ref,
                        out_ref, u_ref, *, kw):
    u = a_ref[...] * _sigmoid(gate_ref[...])
    u_ref[...] = u
    acc = b_ref[...] + w_ref[kw - 1:kw, :] * u
    for t in range(kw - 1):
        acc = acc + w_ref[t:t + 1, :] * st_ref[t]
    out_ref[...] = _bf16(_layernorm_swish(acc, lng_ref[...], lnb_ref[...]))


def _conv_sample(ag, state_t, w, b, lng, lnb):
    nb = ag.shape[0]
    ch = ag.shape[1] // 2
    kw = w.shape[0]
    kern = functools.partial(_conv_sample_kernel, kw=kw)
    return pl.pallas_call(
        kern,
        grid=(1,),
        in_specs=[
            pl.BlockSpec((nb, ch), lambda i: (0, 0)),
            pl.BlockSpec((nb, ch), lambda i: (0, 1)),
            pl.BlockSpec((kw - 1, nb, ch), lambda i: (0, 0, 0)),
            pl.BlockSpec((kw, ch), lambda i: (0, 0)),
            pl.BlockSpec((1, ch), lambda i: (0, 0)),
            pl.BlockSpec((1, ch), lambda i: (0, 0)),
            pl.BlockSpec((1, ch), lambda i: (0, 0)),
        ],
        out_specs=[
            pl.BlockSpec((nb, ch), lambda i: (0, 0)),
            pl.BlockSpec((nb, ch), lambda i: (0, 0)),
        ],
        out_shape=[
            jax.ShapeDtypeStruct((nb, ch), jnp.bfloat16),
            jax.ShapeDtypeStruct((nb, ch), jnp.float32),
        ],
        compiler_params=_params(("arbitrary",)),
        name="conv_sample",
    )(ag, ag, state_t, w, b, lng, lnb)


def _flash_kernel(q_ref, k_ref, v_ref, c_ref, o_ref, *, tq):
    i = pl.program_id(1)
    q = q_ref[...]

    def scores(j):
        r0 = pl.multiple_of(j * tq, tq)
        k = k_ref[pl.ds(r0, tq), :]
        s = lax.dot_general(q, k, _NT, preferred_element_type=jnp.float32)
        return s - c_ref[:, pl.ds(r0, tq)], r0

    def update(s, r0, m, l, acc):
        m_new = jnp.maximum(m, jnp.max(s, axis=-1, keepdims=True))
        alpha = jnp.exp(m - m_new)
        p = jnp.exp(s - m_new)
        l = alpha * l + jnp.sum(p, axis=-1, keepdims=True)
        v = v_ref[pl.ds(r0, tq), :]
        acc = alpha * acc + jnp.dot(_bf16(p), v, preferred_element_type=jnp.float32)
        return m_new, l, acc

    def body(j, carry):
        s, r0 = scores(j)
        return update(s, r0, *carry)

    m0 = jnp.full((tq, 1), -jnp.inf, jnp.float32)
    l0 = jnp.zeros((tq, 1), jnp.float32)
    a0 = jnp.zeros((tq, HEAD_DIM), jnp.float32)
    m, l, acc = lax.fori_loop(0, i, body, (m0, l0, a0))
    s, r0 = scores(i)
    row = lax.broadcasted_iota(jnp.int32, (tq, tq), 0)
    col = lax.broadcasted_iota(jnp.int32, (tq, tq), 1)
    s = jnp.where(row >= col, s, -jnp.inf)
    m, l, acc = update(s, r0, m, l, acc)
    o_ref[...] = _bf16(acc / l)


def _flash_prompt(qkv, c3, *, tq):
    s = qkv.shape[0]
    kern = functools.partial(_flash_kernel, tq=tq)
    return pl.pallas_call(
        kern,
        grid=(N_HEADS, s // tq),
        in_specs=[
            pl.BlockSpec((tq, HEAD_DIM), lambda h, i: (i, h)),
            pl.BlockSpec((s, HEAD_DIM), lambda h, i: (0, N_HEADS + h)),
            pl.BlockSpec((s, HEAD_DIM), lambda h, i: (0, 2 * N_HEADS + h)),
            pl.BlockSpec((None, 1, s), lambda h, i: (h, 0, 0)),
        ],
        out_specs=pl.BlockSpec((tq, HEAD_DIM), lambda h, i: (i, h)),
        out_shape=jax.ShapeDtypeStruct((s, ATTN_WIDTH), jnp.bfloat16),
        compiler_params=_params(("parallel", "arbitrary")),
        name="flash_prompt",
    )(qkv, qkv, qkv, c3)


def _decode_kernel(pt_ref, q_ref, kn_ref, vn_ref, lfn_ref, *refs, npp):
    k_refs = refs[0:npp]
    v_refs = refs[npp:2 * npp]
    lf_refs = refs[2 * npp:3 * npp]
    o_ref = refs[3 * npp]
    m_scr, l_scr, acc_scr, run_scr, lf_scr = refs[3 * npp + 1:]
    g = pl.program_id(1)
    rows = q_ref.shape[0]
    width = k_refs[0].shape[0]
    nch = width // LANE
    q = q_ref[...]

    @pl.when(g == 0)
    def _():
        kn = _bf16(kn_ref[...]).astype(jnp.float32)
        s_new = jnp.sum(q.astype(jnp.float32) * kn, axis=-1, keepdims=True)
        m_scr[...] = s_new
        l_scr[...] = jnp.ones_like(l_scr)
        acc_scr[...] = _bf16(vn_ref[...]).astype(jnp.float32)
        run_scr[...] = lfn_ref[...]

    for p in range(npp):
        lf_scr[p:p + 1, :] = lf_refs[p][...]
    x = lf_scr[...]
    xs = jnp.concatenate([x[:, c * LANE:(c + 1) * LANE] for c in range(nch)], axis=0)
    lane = lax.broadcasted_iota(jnp.int32, xs.shape, 1)
    y = xs
    tot = xs
    for sh in (8, 16, 32, 64):
        y = y + jnp.where(lane + sh < LANE, pltpu.roll(y, LANE - sh, axis=1), 0.0)
        tot = tot + pltpu.roll(tot, sh, axis=1)
    excl = y - xs
    later = jnp.zeros((npp, LANE), jnp.float32)
    pieces = [None] * nch
    for c in reversed(range(nch)):
        pieces[c] = excl[c * npp:(c + 1) * npp, :] + later
        later = later + tot[c * npp:(c + 1) * npp, :]
    within = jnp.concatenate(pieces, axis=1)
    page_tot = later

    hrow = lax.broadcasted_iota(jnp.int32, (rows, width), 0)
    hcol = lax.broadcasted_iota(jnp.int32, (rows, width), 1) & (N_HEADS - 1)
    own = hrow == hcol
    run = run_scr[...]
    m = m_scr[...]
    l = l_scr[...]
    acc = acc_scr[...]
    for p in reversed(range(npp)):
        bias = within[p:p + 1, :] + jnp.concatenate([run] * nch, axis=1)
        kb = _bf16(k_refs[p][...])
        s = lax.dot_general(q, kb, _NT, preferred_element_type=jnp.float32)
        s = jnp.where(own, s + bias, -jnp.inf)
        m_new = jnp.maximum(m, jnp.max(s, axis=-1, keepdims=True))
        alpha = jnp.exp(m - m_new)
        pr = jnp.exp(s - m_new)
        l = alpha * l + jnp.sum(pr, axis=-1, keepdims=True)
        vb = _bf16(v_refs[p][...])
        acc = alpha * acc + jnp.dot(_bf16(pr), vb, preferred_element_type=jnp.float32)
        m = m_new
        run = run + page_tot[p:p + 1, :]
    run_scr[...] = run
    m_scr[...] = m
    l_scr[...] = l
    acc_scr[...] = acc

    @pl.when(g == pl.num_programs(1) - 1)
    def _():
        o_ref[...] = acc / l


def _decode_attention(page_table, q16, k_new16, v_new16, lf_row, kc, vc, lfc):
    nb, n_pages = page_table.shape
    npp = PAGES_PER_STEP
    assert n_pages % npp == 0
    ng = n_pages // npp
    width = kc.shape[1]
    rows = q16.shape[1]

    def page_map(p):
        return lambda b, g, pt: (pt[b, (ng - 1 - g) * npp + p], 0, 0)

    per_b = lambda b, g, pt: (b, 0, 0)
    in_specs = [pl.BlockSpec((None, rows, HEAD_DIM), per_b)] * 3
    in_specs.append(pl.BlockSpec((None, 1, LANE), per_b))
    in_specs += [pl.BlockSpec((None, width, HEAD_DIM), page_map(p)) for p in range(npp)]
    in_specs += [pl.BlockSpec((None, width, HEAD_DIM), page_map(p)) for p in range(npp)]
    in_specs += [pl.BlockSpec((None, 1, width), page_map(p)) for p in range(npp)]
    kern = functools.partial(_decode_kernel, npp=npp)
    grid_spec = pltpu.PrefetchScalarGridSpec(
        num_scalar_prefetch=1,
        grid=(nb, ng),
        in_specs=in_specs,
        out_specs=pl.BlockSpec((None, rows, HEAD_DIM), per_b),
        scratch_shapes=[pltpu.VMEM((rows, 1), jnp.float32),
                        pltpu.VMEM((rows, 1), jnp.float32),
                        pltpu.VMEM((rows, HEAD_DIM), jnp.float32),
                        pltpu.VMEM((1, LANE), jnp.float32),
                        pltpu.VMEM((npp, width), jnp.float32)],
    )
    return pl.pallas_call(
        kern,
        grid_spec=grid_spec,
        out_shape=jax.ShapeDtypeStruct((nb, rows, HEAD_DIM), jnp.float32),
        compiler_params=_params(("arbitrary", "arbitrary")),
        name="decode_attention",
    )(page_table, q16, k_new16, v_new16, lf_row, *([kc] * npp), *([vc] * npp), *([lfc] * npp))


def _outproj_kernel(c_ref, a_ref, x_ref, w1_ref, w2_ref, g_ref, h_ref, hn_ref):
    h = (x_ref[...]
         + jnp.dot(c_ref[...], w1_ref[...], preferred_element_type=jnp.float32)
         + jnp.dot(a_ref[...], w2_ref[...], preferred_element_type=jnp.float32))
    h_ref[...] = h
    ms = jnp.mean(h * h, axis=-1, keepdims=True)
    hn_ref[...] = _bf16(h * lax.rsqrt(ms + EPS) * g_ref[...])


def _outproj(conv_out, attn_out, x, w_out, g, *, tm):
    m, d = x.shape
    half = conv_out.shape[1]
    return pl.pallas_call(
        _outproj_kernel,
        grid=(m // tm,),
        in_specs=[
            pl.BlockSpec((tm, half), lambda i: (i, 0)),
            pl.BlockSpec((tm, half), lambda i: (i, 0)),
            pl.BlockSpec((tm, d), lambda i: (i, 0)),
            pl.BlockSpec((half, d), lambda i: (0, 0)),
            pl.BlockSpec((half, d), lambda i: (1, 0)),
            pl.BlockSpec((1, d), lambda i: (0, 0)),
        ],
        out_specs=[
            pl.BlockSpec((tm, d), lambda i: (i, 0)),
            pl.BlockSpec((tm, d), lambda i: (i, 0)),
        ],
        out_shape=[
            jax.ShapeDtypeStruct((m, d), jnp.float32),
            jax.ShapeDtypeStruct((m, d), jnp.bfloat16),
        ],
        compiler_params=_params(("parallel",)),
        name="outproj",
    )(conv_out, attn_out, x, w_out, w_out, g)


def _final_norm(y, g):
    ms = jnp.mean(y * y, axis=-1, keepdims=True)
    return y * lax.rsqrt(ms + EPS) * g


def _ffn_prompt_kernel(hn_ref, h_ref, wa_ref, wb_ref, wd_ref, cwa_ref, cwb_ref,
                       cba_ref, cbb_ref, gf_ref,
                       y_ref, ta_ref, tb_ref, acc_scr, up_scr, tail_scr, *, tm, tf):
    i = pl.program_id(0)
    j = pl.program_id(1)
    nj = pl.num_programs(1)
    hist = SUBLANE
    slot = pl.multiple_of(j * hist, hist)

    @pl.when(j == 0)
    def _():
        acc_scr[...] = jnp.zeros_like(acc_scr)

    @pl.when(i == 0)
    def _():
        up_scr[0:hist, :] = jnp.zeros((hist, 2 * tf), jnp.float32)

    @pl.when(i > 0)
    def _():
        up_scr[0:hist, :] = tail_scr[pl.ds(slot, hist), :]

    hn = hn_ref[...]
    up_scr[hist:hist + tm, 0:tf] = jnp.dot(hn, wa_ref[...], preferred_element_type=jnp.float32)
    up_scr[hist:hist + tm, tf:2 * tf] = jnp.dot(hn, wb_ref[...], preferred_element_type=jnp.float32)

    def conv(cols, cw_ref, cb_ref):
        return (cb_ref[...]
                + cw_ref[0:1, :] * up_scr[hist - 2:hist - 2 + tm, cols]
                + cw_ref[1:2, :] * up_scr[hist - 1:hist - 1 + tm, cols]
                + cw_ref[2:3, :] * up_scr[hist:hist + tm, cols])

    ca = conv(slice(0, tf), cwa_ref, cba_ref)
    cb = conv(slice(tf, 2 * tf), cwb_ref, cbb_ref)
    act = _bf16(ca * _sigmoid(ca) * cb)
    acc_scr[...] += jnp.dot(act, wd_ref[...], preferred_element_type=jnp.float32)

    tail = up_scr[tm:tm + hist, :]
    tail_scr[pl.ds(slot, hist), :] = tail

    @pl.when(i == pl.num_programs(0) - 1)
    def _():
        cols = pl.ds(pl.multiple_of(j * tf, tf), tf)
        ta_ref[:, cols] = tail[:, 0:tf]
        tb_ref[:, cols] = tail[:, tf:2 * tf]

    @pl.when(j == nj - 1)
    def _():
        y_ref[...] = _final_norm(h_ref[...] + acc_scr[...], gf_ref[...])


def _ffn_prompt(hn, h, w_up, w_down, cw, cb, gf, *, tm, tf):
    m, d = h.shape
    dff = w_down.shape[0]
    assert dff % tf == 0 and m % tm == 0
    nj = dff // tf
    kern = functools.partial(_ffn_prompt_kernel, tm=tm, tf=tf)
    return pl.pallas_call(
        kern,
        grid=(m // tm, nj),
        in_specs=[
            pl.BlockSpec((tm, d), lambda i, j: (i, 0)),
            pl.BlockSpec((tm, d), lambda i, j: (i, 0)),
            pl.BlockSpec((d, tf), lambda i, j: (0, j)),
            pl.BlockSpec((d, tf), lambda i, j: (0, j + nj)),
            pl.BlockSpec((tf, d), lambda i, j: (j, 0)),
            pl.BlockSpec((3, tf), lambda i, j: (0, j)),
            pl.BlockSpec((3, tf), lambda i, j: (0, j + nj)),
            pl.BlockSpec((1, tf), lambda i, j: (0, j)),
            pl.BlockSpec((1, tf), lambda i, j: (0, j + nj)),
            pl.BlockSpec((1, d), lambda i, j: (0, 0)),
        ],
        out_specs=[
            pl.BlockSpec((tm, d), lambda i, j: (i, 0)),
            pl.BlockSpec((SUBLANE, dff), lambda i, j: (0, 0)),
            pl.BlockSpec((SUBLANE, dff), lambda i, j: (0, 0)),
        ],
        out_shape=[
            jax.ShapeDtypeStruct((m, d), jnp.float32),
            jax.ShapeDtypeStruct((SUBLANE, dff), jnp.float32),
            jax.ShapeDtypeStruct((SUBLANE, dff), jnp.float32),
        ],
        scratch_shapes=[pltpu.VMEM((tm, d), jnp.float32),
                        pltpu.VMEM((SUBLANE + tm, 2 * tf), jnp.float32),
                        pltpu.VMEM((nj * SUBLANE, 2 * tf), jnp.float32)],
        compiler_params=_params(("arbitrary", "arbitrary")),
        name="ffn_prompt",
    )(hn, h, w_up, w_up, w_down, cw, cw, cb, cb, gf)


def _ffn_sample_kernel(hn_ref, h_ref, wa_ref, wb_ref, wd_ref, sta_ref, stb_ref,
                       cwa_ref, cwb_ref, cba_ref, cbb_ref, gf_ref,
                       y_ref, ua_ref, ub_ref, acc_scr):
    j = pl.program_id(0)

    @pl.when(j == 0)
    def _():
        acc_scr[...] = jnp.zeros_like(acc_scr)

    hn = hn_ref[...]
    ua = jnp.dot(hn, wa_ref[...], preferred_element_type=jnp.float32)
    ub = jnp.dot(hn, wb_ref[...], preferred_element_type=jnp.float32)
    ua_ref[...] = ua
    ub_ref[...] = ub
    ca = cba_ref[...] + cwa_ref[0:1, :] * sta_ref[0] + cwa_ref[1:2, :] * sta_ref[1] + cwa_ref[2:3, :] * ua
    cb = cbb_ref[...] + cwb_ref[0:1, :] * stb_ref[0] + cwb_ref[1:2, :] * stb_ref[1] + cwb_ref[2:3, :] * ub
    act = _bf16(ca * _sigmoid(ca) * cb)
    acc_scr[...] += jnp.dot(act, wd_ref[...], preferred_element_type=jnp.float32)

    @pl.when(j == pl.num_programs(0) - 1)
    def _():
        y_ref[...] = _final_norm(h_ref[...] + acc_scr[...], gf_ref[...])


def _ffn_sample(hn, h, w_up, w_down, state_t, cw, cb, gf, *, tf):
    m, d = h.shape
    dff = w_down.shape[0]
    nj = dff // tf
    return pl.pallas_call(
        _ffn_sample_kernel,
        grid=(nj,),
        in_specs=[
            pl.BlockSpec((m, d), lambda j: (0, 0)),
            pl.BlockSpec((m, d), lambda j: (0, 0)),
            pl.BlockSpec((d, tf), lambda j: (0, j)),
            pl.BlockSpec((d, tf), lambda j: (0, j + nj)),
            pl.BlockSpec((tf, d), lambda j: (j, 0)),
            pl.BlockSpec((2, m, tf), lambda j: (0, 0, j)),
            pl.BlockSpec((2, m, tf), lambda j: (0, 0, j + nj)),
            pl.BlockSpec((3, tf), lambda j: (0, j)),
            pl.BlockSpec((3, tf), lambda j: (0, j + nj)),
            pl.BlockSpec((1, tf), lambda j: (0, j)),
            pl.BlockSpec((1, tf), lambda j: (0, j + nj)),
            pl.BlockSpec((1, d), lambda j: (0, 0)),
        ],
        out_specs=[
            pl.BlockSpec((m, d), lambda j: (0, 0)),
            pl.BlockSpec((m, tf), lambda j: (0, j)),
            pl.BlockSpec((m, tf), lambda j: (0, j)),
        ],
        out_shape=[
            jax.ShapeDtypeStruct((m, d), jnp.float32),
            jax.ShapeDtypeStruct((m, dff), jnp.float32),
            jax.ShapeDtypeStruct((m, dff), jnp.float32),
        ],
        scratch_shapes=[pltpu.VMEM((m, d), jnp.float32)],
        compiler_params=_params(("arbitrary",)),
        name="ffn_sample",
    )(hn, h, w_up, w_up, w_down, state_t, state_t, cw, cw, cb, cb, gf)


def _pick(n, prefs):
    for p in prefs:
        if n % p == 0:
            return p
    return n


def kernel(x_prompt, x_sample, cache_k, cache_v, cache_logf, state_conv, state_ffn, page_table,
           norm_mix_g, w_in, b_f, conv_dw_w, conv_dw_b, conv_ln_g, conv_ln_b, w_out,
           norm_ffn_g, w_up, ffn_dw_w, ffn_dw_b, w_down, norm_final_g):
    f32 = jnp.float32
    depth = w_in.shape[0]
    assert depth == 1 and x_prompt.shape[0] == 1 and x_sample.shape[1] == 1
    s, d = x_prompt.shape[1], x_prompt.shape[2]
    nb = x_sample.shape[0]
    n_pool, page = cache_k.shape[1], cache_k.shape[2]
    assert cache_k.shape[3:] == (N_HEADS, HEAD_DIM)
    dff = w_down.shape[1]
    conv_ch = conv_dw_w.shape[2]
    assert conv_ch == ATTN_WIDTH
    n_main = 2 * conv_ch + 3 * ATTN_WIDTH

    w_in_bf = _bf16(w_in[0])
    w_main = w_in_bf[:, :n_main]
    pad_rows = ((0, HEAD_ROWS - N_HEADS), (0, 0))
    wf_t = jnp.pad(jnp.transpose(w_in_bf[:, n_main:]), pad_rows)
    bf_col = jnp.pad(b_f[0].reshape(N_HEADS, 1), pad_rows)
    w_out_bf = _bf16(w_out[0])
    w_up_bf = _bf16(w_up[0])
    w_down_bf = _bf16(w_down[0])
    row = lambda v: v.reshape(1, -1)
    g_mix, g_ffn, g_fin = row(norm_mix_g[0]), row(norm_ffn_g[0]), row(norm_final_g)
    cw, cb_, lng, lnb = conv_dw_w[0], row(conv_dw_b[0]), row(conv_ln_g[0]), row(conv_ln_b[0])
    fcw, fcb = ffn_dw_w[0], row(ffn_dw_b[0])
    kw = cw.shape[0]
    tri = _bf16(jnp.triu(jnp.ones((SCAN_CHUNK, SCAN_CHUNK), f32)))

    tf = _pick(dff, (512, 256, 128))

    xp = x_prompt[0]
    tm_in = _pick(s, (512, 256))
    ag_p, kv_p, qkv_p, lft_p, ct_p = _inproj(xp, g_mix, w_main, wf_t, bf_col, tri,
                                             tm=tm_in, with_scan=True)
    conv_p, utail_p = _conv_prompt(ag_p, cw, cb_, lng, lnb, tb=_pick(s, (256, 128)))
    attn_p = _flash_prompt(qkv_p, ct_p[:N_HEADS].reshape(N_HEADS, 1, s), tq=_pick(s, (512, 256, 128)))
    h_p, hn_p = _outproj(conv_p, attn_p, xp, w_out_bf, g_ffn, tm=_pick(s, (512, 256)))
    y_p, ta_p, tb_p = _ffn_prompt(hn_p, h_p, w_up_bf, w_down_bf, fcw, fcb, g_fin,
                                  tm=_pick(s, (512, 256)), tf=tf)

    y_prompt = y_p[None]
    k_prompt = kv_p[:, :ATTN_WIDTH].reshape(1, 1, s, N_HEADS, HEAD_DIM)
    v_prompt = kv_p[:, ATTN_WIDTH:].reshape(1, 1, s, N_HEADS, HEAD_DIM)
    logf_prompt = jnp.transpose(lft_p[:N_HEADS]).reshape(1, 1, s, N_HEADS)
    conv_prompt = utail_p[utail_p.shape[0] - (kw - 1):][None, None]
    ffn_prompt = jnp.concatenate([ta_p[SUBLANE - 2:], tb_p[SUBLANE - 2:]], axis=1)[None, None]

    xs = x_sample[:, 0, :]
    ag_s, kv_s, qkv_s, lft_s, _ = _inproj(xs, g_mix, w_main, wf_t, bf_col, tri,
                                          tm=nb, with_scan=False)
    conv_s, u_s = _conv_sample(ag_s, jnp.swapaxes(state_conv[0], 0, 1), cw, cb_, lng, lnb)

    pad_heads = lambda a: jnp.pad(a.reshape(nb, N_HEADS, HEAD_DIM), ((0, 0), (0, N_HEADS), (0, 0)))
    q16 = pad_heads(qkv_s[:, :ATTN_WIDTH])
    kn16 = pad_heads(kv_s[:, :ATTN_WIDTH])
    vn16 = pad_heads(kv_s[:, ATTN_WIDTH:])
    lf_s = jnp.transpose(lft_s[:N_HEADS])
    lf_row = jnp.tile(lf_s, (1, LANE // N_HEADS)).reshape(nb, 1, LANE)
    kc = cache_k[0].reshape(n_pool, page * N_HEADS, HEAD_DIM)
    vc = cache_v[0].reshape(n_pool, page * N_HEADS, HEAD_DIM)
    lfc = cache_logf[0].astype(f32).reshape(n_pool, 1, page * N_HEADS)
    attn_s16 = _decode_attention(page_table, q16, kn16, vn16, lf_row, kc, vc, lfc)
    attn_s = _bf16(attn_s16[:, :N_HEADS, :].reshape(nb, ATTN_WIDTH))

    h_s, hn_s = _outproj(conv_s, attn_s, xs, w_out_bf, g_ffn, tm=nb)
    y_s, ua_s, ub_s = _ffn_sample(hn_s, h_s, w_up_bf, w_down_bf,
                                  jnp.swapaxes(state_ffn[0], 0, 1), fcw, fcb, g_fin, tf=tf)

    y_sample = y_s[:, None, :]
    k_sample = kv_s[:, :ATTN_WIDTH].reshape(1, nb, 1, N_HEADS, HEAD_DIM)
    v_sample = kv_s[:, ATTN_WIDTH:].reshape(1, nb, 1, N_HEADS, HEAD_DIM)
    logf_sample = lf_s.reshape(1, nb, 1, N_HEADS)
    conv_sample = jnp.concatenate([state_conv[0][:, 1:, :], u_s[:, None, :]], axis=1)[None]
    up_s = jnp.concatenate([ua_s, ub_s], axis=1)
    ffn_sample = jnp.concatenate([state_ffn[0][:, 1:, :], up_s[:, None, :]], axis=1)[None]

    return (y_prompt, y_sample, k_prompt, v_prompt, logf_prompt, conv_prompt, ffn_prompt,
            k_sample, v_sample, logf_sample, conv_sample, ffn_sample)
```

```python
import functools
import math

import jax
import jax.numpy as jnp
from jax import lax
from jax.experimental import pallas as pl
from jax.experimental.pallas import tpu as pltpu

EPS = 1e-6
LANE = 128
SUBLANE = 8
MXU_DIM = 256
HEAD_DIM = 128
N_HEADS = 8
ATTN_WIDTH = N_HEADS * HEAD_DIM
HEAD_ROWS = 16
QK_DEPTH = MXU_DIM
V_ROWS = HEAD_DIM + 16
SCAN_CHUNK = 256
PAGES_PER_STEP = 8
VMEM_LIMIT = 56 * 1024 * 1024
LOG2E = math.log2(math.e)
NEG = -1e30

_NT = (((1,), (1,)), ((), ()))


def _bf16(x):
    return x.astype(jnp.bfloat16)


def _dot(a, b):
    return jnp.dot(a, b, preferred_element_type=jnp.float32)


def _split3(x):
    hi = _bf16(x)
    r1 = x - hi.astype(jnp.float32)
    mid = _bf16(r1)
    lo = _bf16(r1 - mid.astype(jnp.float32))
    return hi, mid, lo


def _log_sigmoid(x):
    return jnp.minimum(x, 0.0) - jnp.log1p(jnp.exp(-jnp.abs(x)))


def _sigmoid(x):
    return 1.0 / (1.0 + jnp.exp(-x))


def _rms(x, g):
    ms = jnp.mean(x * x, axis=-1, keepdims=True)
    return x * lax.rsqrt(ms + EPS) * g


def _params(sem):
    return pltpu.CompilerParams(dimension_semantics=sem, vmem_limit_bytes=VMEM_LIMIT)


def _inproj_prompt_kernel(x_ref, g_ref, w_ref, wt_ref, wf_ref, bf_ref, tril_ref, psel_ref,
                          ag_ref, kv_ref, qat_ref, ka_ref, vat_ref, lf_ref,
                          xn_scr, aug_scr, carry_scr, *, tm, tn, q_scale):
    i = pl.program_id(0)
    j = pl.program_id(1)
    hpt = tn // HEAD_DIM
    tps = ATTN_WIDTH // tn

    @pl.when(j == 0)
    def _():
        xn_scr[...] = _bf16(_rms(x_ref[...], g_ref[...]))
        lf = _log_sigmoid(_dot(xn_scr[...], wf_ref[...]) + bf_ref[...])
        lf_ref[...] = lf

        @pl.when(i == 0)
        def _():
            carry_scr[...] = jnp.zeros_like(carry_scr)

        carry = carry_scr[0:1, :]
        tril = tril_ref[...]
        for c in range(tm // SCAN_CHUNK):
            rows = slice(c * SCAN_CHUNK, (c + 1) * SCAN_CHUNK)
            hi, mid, lo = _split3(lf[rows, :])
            cs = _dot(tril, hi) + _dot(tril, mid) + _dot(tril, lo) + carry
            h2, m2, l2 = _split3(cs * (-LOG2E))
            aug = (_dot(h2, psel_ref[0:LANE, :]) + _dot(m2, psel_ref[LANE:2 * LANE, :])
                   + _dot(l2, psel_ref[2 * LANE:3 * LANE, :]))
            aug_scr[rows, :] = _bf16(aug)
            carry = cs[SCAN_CHUNK - 1:SCAN_CHUNK, :]
        carry_scr[...] = jnp.broadcast_to(carry, carry_scr.shape)

    def nn():
        return _dot(xn_scr[...], w_ref[...])

    def nt():
        return lax.dot_general(wt_ref[...], xn_scr[...], _NT,
                               preferred_element_type=jnp.float32)

    @pl.when(j < 2 * tps)
    def _():
        ag_ref[...] = nn()

    for t in range(tps):
        @pl.when(j == 2 * tps + t)
        def _(t=t):
            zt = nt() * q_scale
            ones = (lax.broadcasted_iota(jnp.int32, (QK_DEPTH - HEAD_DIM, tm), 0) < 3
                    ).astype(jnp.bfloat16)
            for hh in range(hpt):
                r = (t * hpt + hh) * QK_DEPTH
                qat_ref[r:r + HEAD_DIM, :] = _bf16(zt[hh * HEAD_DIM:(hh + 1) * HEAD_DIM, :])
                qat_ref[r + HEAD_DIM:r + QK_DEPTH, :] = ones

        @pl.when(j == 3 * tps + t)
        def _(t=t):
            z = nn()
            kv_ref[...] = z
            for hh in range(hpt):
                h = t * hpt + hh
                c = h * QK_DEPTH
                ka_ref[:, c:c + HEAD_DIM] = _bf16(z[:, hh * HEAD_DIM:(hh + 1) * HEAD_DIM])
                ka_ref[:, c + HEAD_DIM:c + QK_DEPTH] = aug_scr[:, h * LANE:(h + 1) * LANE]

        @pl.when(j == 4 * tps + t)
        def _(t=t):
            kv_ref[...] = nn()
            zt = nt()
            ones = (lax.broadcasted_iota(jnp.int32, (V_ROWS - HEAD_DIM, tm), 0) == 0
                    ).astype(jnp.bfloat16)
            for hh in range(hpt):
                r = (t * hpt + hh) * V_ROWS
                vat_ref[r:r + HEAD_DIM, :] = _bf16(zt[hh * HEAD_DIM:(hh + 1) * HEAD_DIM, :])
                vat_ref[r + HEAD_DIM:r + V_ROWS, :] = ones


def _inproj_prompt(x, g, w_main, wt_qv, wf_pad, bf_row, tril, psel, *, tm, tn):
    m, d = x.shape
    tps = ATTN_WIDTH // tn
    nj = 5 * tps
    assert w_main.shape == (d, 5 * ATTN_WIDTH) and m % tm == 0 and tm % SCAN_CHUNK == 0
    kern = functools.partial(_inproj_prompt_kernel, tm=tm, tn=tn,
                             q_scale=HEAD_DIM ** -0.5 * LOG2E)

    def w_map(i, j):
        return (0, jnp.where((j >= 2 * tps) & (j < 3 * tps), 2 * tps - 1, j))

    def wt_map(i, j):
        t = jnp.clip(j - 2 * tps, 0, tps - 1) + jnp.where(j >= 3 * tps, 1, 0) \
            + jnp.clip(j - 4 * tps, 0, tps - 1)
        return (jnp.minimum(t, 2 * tps - 1), 0)

    return pl.pallas_call(
        kern,
        grid=(m // tm, nj),
        in_specs=[
            pl.BlockSpec((tm, d), lambda i, j: (i, 0)),
            pl.BlockSpec((1, d), lambda i, j: (0, 0)),
            pl.BlockSpec((d, tn), w_map),
            pl.BlockSpec((tn, d), wt_map),
            pl.BlockSpec((d, LANE), lambda i, j: (0, 0)),
            pl.BlockSpec((1, LANE), lambda i, j: (0, 0)),
            pl.BlockSpec((SCAN_CHUNK, SCAN_CHUNK), lambda i, j: (0, 0)),
            pl.BlockSpec((3 * LANE, ATTN_WIDTH), lambda i, j: (0, 0)),
        ],
        out_specs=[
            pl.BlockSpec((tm, tn), lambda i, j: (i, jnp.minimum(j, 2 * tps - 1))),
            pl.BlockSpec((tm, tn), lambda i, j: (i, jnp.clip(j - 3 * tps, 0, 2 * tps - 1))),
            pl.BlockSpec((N_HEADS * QK_DEPTH, tm), lambda i, j: (0, i)),
            pl.BlockSpec((tm, N_HEADS * QK_DEPTH), lambda i, j: (i, 0)),
            pl.BlockSpec((N_HEADS * V_ROWS, tm), lambda i, j: (0, i)),
            pl.BlockSpec((tm, LANE), lambda i, j: (i, 0)),
        ],
        out_shape=[
            jax.ShapeDtypeStruct((m, 2 * ATTN_WIDTH), jnp.float32),
            jax.ShapeDtypeStruct((m, 2 * ATTN_WIDTH), jnp.float32),
            jax.ShapeDtypeStruct((N_HEADS * QK_DEPTH, m), jnp.bfloat16),
            jax.ShapeDtypeStruct((m, N_HEADS * QK_DEPTH), jnp.bfloat16),
            jax.ShapeDtypeStruct((N_HEADS * V_ROWS, m), jnp.bfloat16),
            jax.ShapeDtypeStruct((m, LANE), jnp.float32),
        ],
        scratch_shapes=[pltpu.VMEM((tm, d), jnp.bfloat16),
                        pltpu.VMEM((tm, ATTN_WIDTH), jnp.bfloat16),
                        pltpu.VMEM((SUBLANE, LANE), jnp.float32)],
        compiler_params=_params(("arbitrary", "arbitrary")),
        name="inproj_prompt",
    )(x, g, w_main, wt_qv, wf_pad, bf_row, tril, psel)


def _inproj_sample_kernel(x_ref, g_ref, w_ref, wf_ref, bf_ref,
                          ag_ref, kv_ref, q_ref, lf_ref, xn_scr, *, q_scale):
    j = pl.program_id(0)

    @pl.when(j == 0)
    def _():
        xn_scr[...] = _bf16(_rms(x_ref[...], g_ref[...]))
        lf_ref[...] = _log_sigmoid(_dot(xn_scr[...], wf_ref[...]) + bf_ref[...])

    z = _dot(xn_scr[...], w_ref[...])

    @pl.when(j < 2)
    def _():
        ag_ref[...] = z

    @pl.when(j == 2)
    def _():
        q_ref[...] = _bf16(z * q_scale)

    @pl.when(j > 2)
    def _():
        kv_ref[...] = z


def _inproj_sample(x, g, w_main, wf_pad, bf_row):
    m, d = x.shape
    tn = ATTN_WIDTH
    kern = functools.partial(_inproj_sample_kernel, q_scale=HEAD_DIM ** -0.5)
    return pl.pallas_call(
        kern,
        grid=(5,),
        in_specs=[
            pl.BlockSpec((m, d), lambda j: (0, 0)),
            pl.BlockSpec((1, d), lambda j: (0, 0)),
            pl.BlockSpec((d, tn), lambda j: (0, j)),
            pl.BlockSpec((d, LANE), lambda j: (0, 0)),
            pl.BlockSpec((1, LANE), lambda j: (0, 0)),
        ],
        out_specs=[
            pl.BlockSpec((m, tn), lambda j: (0, jnp.minimum(j, 1))),
            pl.BlockSpec((m, tn), lambda j: (0, jnp.clip(j - 3, 0, 1))),
            pl.BlockSpec((m, tn), lambda j: (0, 0)),
            pl.BlockSpec((m, LANE), lambda j: (0, 0)),
        ],
        out_shape=[
            jax.ShapeDtypeStruct((m, 2 * tn), jnp.float32),
            jax.ShapeDtypeStruct((m, 2 * tn), jnp.float32),
            jax.ShapeDtypeStruct((m, tn), jnp.bfloat16),
            jax.ShapeDtypeStruct((m, LANE), jnp.float32),
        ],
        scratch_shapes=[pltpu.VMEM((m, d), jnp.bfloat16)],
        compiler_params=_params(("arbitrary",)),
        name="inproj_sample",
    )(x, g, w_main, wf_pad, bf_row)


def _layernorm_swish(y, g, b):
    mu = jnp.mean(y, axis=-1, keepdims=True)
    yc = y - mu
    var = jnp.mean(yc * yc, axis=-1, keepdims=True)
    yn = yc * lax.rsqrt(var + EPS) * g + b
    return yn * _sigmoid(yn)


def _conv_prompt_kernel(a_ref, gate_ref, w_ref, b_ref, lng_ref, lnb_ref,
                        out_ref, tail_ref, uext_scr, ush_scr, y_scr, *, tb, kw, hist):
    i = pl.program_id(0)

    @pl.when(i == 0)
    def _():
        uext_scr[0:hist, :] = jnp.zeros((hist, uext_scr.shape[1]), jnp.float32)

    uext_scr[hist:hist + tb, :] = a_ref[...] * _sigmoid(gate_ref[...])
    span = ush_scr.shape[1]
    for sh in range(1, SUBLANE):
        ush_scr[sh - 1] = uext_scr[sh:sh + span, :]
    ch = a_ref.shape[1]
    rows = 128
    off = hist - (kw - 1)
    nrc = tb // rows
    for cc in range(ch // LANE):
        cs = slice(cc * LANE, (cc + 1) * LANE)
        accs = [jnp.broadcast_to(b_ref[:, cs], (rows, LANE))] * nrc
        for t in range(kw):
            sh = (off + t) % SUBLANE
            wt = w_ref[t:t + 1, cs]
            for rc in range(nrc):
                r0 = rc * rows + (off + t) - sh
                if sh == 0:
                    u = uext_scr[r0:r0 + rows, cs]
                else:
                    u = ush_scr[sh - 1, r0:r0 + rows, cs]
                accs[rc] = accs[rc] + wt * u
        for rc in range(nrc):
            y_scr[rc * rows:(rc + 1) * rows, cs] = accs[rc]
    out_ref[...] = _bf16(_layernorm_swish(y_scr[...], lng_ref[...], lnb_ref[...]))
    tail = uext_scr[tb:tb + hist, :]
    uext_scr[0:hist, :] = tail
    tail_ref[...] = tail


def _conv_prompt(ag, w, b, lng, lnb, *, tb):
    s = ag.shape[0]
    ch = ag.shape[1] // 2
    kw = w.shape[0]
    hist = 32
    assert kw - 1 <= hist and s % tb == 0
    kern = functools.partial(_conv_prompt_kernel, tb=tb, kw=kw, hist=hist)
    return pl.pallas_call(
        kern,
        grid=(s // tb,),
        in_specs=[
            pl.BlockSpec((tb, ch), lambda i: (i, 0)),
            pl.BlockSpec((tb, ch), lambda i: (i, 1)),
            pl.BlockSpec((kw, ch), lambda i: (0, 0)),
            pl.BlockSpec((1, ch), lambda i: (0, 0)),
            pl.BlockSpec((1, ch), lambda i: (0, 0)),
            pl.BlockSpec((1, ch), lambda i: (0, 0)),
        ],
        out_specs=[
            pl.BlockSpec((tb, ch), lambda i: (i, 0)),
            pl.BlockSpec((hist, ch), lambda i: (0, 0)),
        ],
        out_shape=[
            jax.ShapeDtypeStruct((s, ch), jnp.bfloat16),
            jax.ShapeDtypeStruct((hist, ch), jnp.float32),
        ],
        scratch_shapes=[pltpu.VMEM((hist + tb, ch), jnp.float32),
                        pltpu.VMEM((SUBLANE - 1, hist + tb - SUBLANE, ch), jnp.float32),
                        pltpu.VMEM((tb, ch), jnp.float32)],
        compiler_params=_params(("arbitrary",)),
        name="conv_prompt",
    )(ag, ag, w, b, lng, lnb)


def _conv_sample_kernel(a_ref, gate_ref, st_ref, w_ref, b_ref, lng_ref, lnb_ref,
                        out_ref, u_ref, *, kw):
    u = a_ref[...] * _sigmoid(gate_ref[...])
    u_ref[...] = u
    acc = b_ref[...] + w_ref[kw - 1:kw, :] * u
    for t in range(kw - 1):
        acc = acc + w_ref[t:t + 1, :] * st_ref[t]
    out_ref[...] = _bf16(_layernorm_swish(acc, lng_ref[...], lnb_ref[...]))


def _conv_sample(ag, state_t, w, b, lng, lnb):
    nb = ag.shape[0]
    ch = ag.shape[1] // 2
    kw = w.shape[0]
    kern = functools.partial(_conv_sample_kernel, kw=kw)
    return pl.pallas_call(
        kern,
        grid=(1,),
        in_specs=[
            pl.BlockSpec((nb, ch), lambda i: (0, 0)),
            pl.BlockSpec((nb, ch), lambda i: (0, 1)),
            pl.BlockSpec((kw - 1, nb, ch), lambda i: (0, 0, 0)),
            pl.BlockSpec((kw, ch), lambda i: (0, 0)),
            pl.BlockSpec((1, ch), lambda i: (0, 0)),
            pl.BlockSpec((1, ch), lambda i: (0, 0)),
            pl.BlockSpec((1, ch), lambda i: (0, 0)),
        ],
        out_specs=[
            pl.BlockSpec((nb, ch), lambda i: (0, 0)),
            pl.BlockSpec((nb, ch), lambda i: (0, 0)),
        ],
        out_shape=[
            jax.ShapeDtypeStruct((nb, ch), jnp.bfloat16),
            jax.ShapeDtypeStruct((nb, ch), jnp.float32),
        ],
        compiler_params=_params(("arbitrary",)),
        name="conv_sample",
    )(ag, ag, state_t, w, b, lng, lnb)


def _flash_kernel(qat_ref, ka_ref, vat_ref, o_ref, s_scr, acc_scr, *, tq):
    i = pl.program_id(1)
    qat = qat_ref[...]
    acc_scr[...] = jnp.zeros_like(acc_scr)

    def scores(j, slot, masked):
        r0 = pl.multiple_of(j * tq, tq)
        st = _dot(ka_ref[pl.ds(r0, tq), :], qat)
        if masked:
            key = lax.broadcasted_iota(jnp.int32, (tq, tq), 0)
            qry = lax.broadcasted_iota(jnp.int32, (tq, tq), 1)
            st = jnp.where(key <= qry, st, NEG)
        s_scr[slot] = st
        return jnp.max(st, axis=0, keepdims=True)

    def accumulate(j, slot, m_old, mx):
        r0 = pl.multiple_of(j * tq, tq)
        m_new = jnp.maximum(m_old, mx)
        alpha = jnp.exp2(m_old - m_new)
        pt = _bf16(jnp.exp2(s_scr[slot] - m_new))
        acc_scr[...] = alpha * acc_scr[...] + _dot(vat_ref[:, pl.ds(r0, tq)], pt)
        return m_new

    m0 = jnp.full((1, tq), NEG, jnp.float32)

    @pl.when(i == 0)
    def _():
        accumulate(0, 0, m0, scores(0, 0, True))

    pairs = (i - 1) // 2

    def body(p, carry):
        m, mx0 = carry
        j = 2 * p
        mx1 = scores(j + 1, 1, False)
        m = accumulate(j, 0, m, mx0)
        mx0 = scores(j + 2, 0, False)
        m = accumulate(j + 1, 1, m, mx1)
        return m, mx0

    @pl.when((i > 0) & (i % 2 == 1))
    def _():
        m, mx0 = lax.fori_loop(0, pairs, body, (m0, scores(0, 0, False)))
        mx1 = scores(i, 1, True)
        m = accumulate(i - 1, 0, m, mx0)
        accumulate(i, 1, m, mx1)

    @pl.when((i > 0) & (i % 2 == 0))
    def _():
        m, mx0 = lax.fori_loop(0, pairs, body, (m0, scores(0, 0, False)))
        mx1 = scores(i - 1, 1, False)
        m = accumulate(i - 2, 0, m, mx0)
        mx0 = scores(i, 0, True)
        m = accumulate(i - 1, 1, m, mx1)
        accumulate(i, 0, m, mx0)

    acc = acc_scr[...]
    out_t = acc[0:HEAD_DIM, :] / acc[HEAD_DIM:HEAD_DIM + 1, :]
    o_ref[...] = _bf16(jnp.transpose(out_t))


def _flash_prompt(qat, ka, vat, *, tq):
    s = ka.shape[0]
    kern = functools.partial(_flash_kernel, tq=tq)
    return pl.pallas_call(
        kern,
        grid=(N_HEADS, s // tq),
        in_specs=[
            pl.BlockSpec((QK_DEPTH, tq), lambda h, i: (h, i)),
            pl.BlockSpec((s, QK_DEPTH), lambda h, i: (0, h)),
            pl.BlockSpec((V_ROWS, s), lambda h, i: (h, 0)),
        ],
        out_specs=pl.BlockSpec((tq, HEAD_DIM), lambda h, i: (i, h)),
        out_shape=jax.ShapeDtypeStruct((s, ATTN_WIDTH), jnp.bfloat16),
        scratch_shapes=[pltpu.VMEM((2, tq, tq), jnp.float32),
                        pltpu.VMEM((V_ROWS, tq), jnp.float32)],
        compiler_params=_params(("parallel", "arbitrary")),
        name="flash_prompt",
    )(qat, ka, vat)


def _decode_kernel(pt_ref, q_ref, kn_ref, vn_ref, lfn_ref, *refs, npp):
    k_refs = refs[0:npp]
    v_refs = refs[npp:2 * npp]
    lf_refs = refs[2 * npp:3 * npp]
    o_ref = refs[3 * npp]
    m_scr, l_scr, acc_scr, run_scr, lf_scr = refs[3 * npp + 1:]
    g = pl.program_id(1)
    rows = q_ref.shape[0]
    width = k_refs[0].shape[0]
    nch = width // LANE
    q = q_ref[...]

    @pl.when(g == 0)
    def _():
        kn = _bf16(kn_ref[...]).astype(jnp.float32)
        s_new = jnp.sum(q.astype(jnp.float32) * kn, axis=-1, keepdims=True)
        m_scr[...] = s_new
        l_scr[...] = jnp.ones_like(l_scr)
        acc_scr[...] = _bf16(vn_ref[...]).astype(jnp.float32)
        run_scr[...] = lfn_ref[...]

    for p in range(npp):
        lf_scr[p:p + 1, :] = lf_refs[p][...]
    x = lf_scr[...]
    xs = jnp.concatenate([x[:, c * LANE:(c + 1) * LANE] for c in range(nch)], axis=0)
    lane = lax.broadcasted_iota(jnp.int32, xs.shape, 1)
    y = xs
    tot = xs
    for sh in (8, 16, 32, 64):
        y = y + jnp.where(lane + sh < LANE, pltpu.roll(y, LANE - sh, axis=1), 0.0)
        tot = tot + pltpu.roll(tot, sh, axis=1)
    excl = y - xs
    later = jnp.zeros((npp, LANE), jnp.float32)
    pieces = [None] * nch
    for c in reversed(range(nch)):
        pieces[c] = excl[c * npp:(c + 1) * npp, :] + later
        later = later + tot[c * npp:(c + 1) * npp, :]
    within = jnp.concatenate(pieces, axis=1)
    page_tot = later

    hrow = lax.broadcasted_iota(jnp.int32, (rows, width), 0)
    hcol = lax.broadcasted_iota(jnp.int32, (rows, width), 1) & (N_HEADS - 1)
    own = hrow == hcol
    run = run_scr[...]
    scores = [None] * npp
    for p in reversed(range(npp)):
        bias = within[p:p + 1, :] + jnp.concatenate([run] * nch, axis=1)
        s = lax.dot_general(q, _bf16(k_refs[p][...]), _NT, preferred_element_type=jnp.float32)
        scores[p] = jnp.where(own, s + bias, NEG)
        run = run + page_tot[p:p + 1, :]
    run_scr[...] = run
    smax = scores[0]
    for p in range(1, npp):
        smax = jnp.maximum(smax, scores[p])
    m_old = m_scr[...]
    m_new = jnp.maximum(m_old, jnp.max(smax, axis=-1, keepdims=True))
    alpha = jnp.exp(m_old - m_new)
    psum = jnp.zeros((rows, width), jnp.float32)
    acc = alpha * acc_scr[...]
    for p in range(npp):
        pr = jnp.exp(scores[p] - m_new)
        psum = psum + pr
        acc = acc + _dot(_bf16(pr), _bf16(v_refs[p][...]))
    l_new = alpha * l_scr[...] + jnp.sum(psum, axis=-1, keepdims=True)
    m_scr[...] = m_new
    l_scr[...] = l_new
    acc_scr[...] = acc

    @pl.when(g == pl.num_programs(1) - 1)
    def _():
        o_ref[...] = acc / l_new


def _decode_attention(page_table, q16, k_new16, v_new16, lf_row, kc, vc, lfc):
    nb, n_pages = page_table.shape
    npp = PAGES_PER_STEP
    assert n_pages % npp == 0
    ng = n_pages // npp
    width = kc.shape[1]
    rows = q16.shape[1]

    def page_map(p):
        return lambda b, g, pt: (pt[b, (ng - 1 - g) * npp + p], 0, 0)

    per_b = lambda b, g, pt: (b, 0, 0)
    in_specs = [pl.BlockSpec((None, rows, HEAD_DIM), per_b)] * 3
    in_specs.append(pl.BlockSpec((None, 1, LANE), per_b))
    in_specs += [pl.BlockSpec((None, width, HEAD_DIM), page_map(p)) for p in range(npp)]
    in_specs += [pl.BlockSpec((None, width, HEAD_DIM), page_map(p)) for p in range(npp)]
    in_specs += [pl.BlockSpec((None, 1, width), page_map(p)) for p in range(npp)]
    kern = functools.partial(_decode_kernel, npp=npp)
    grid_spec = pltpu.PrefetchScalarGridSpec(
        num_scalar_prefetch=1,
        grid=(nb, ng),
        in_specs=in_specs,
        out_specs=pl.BlockSpec((None, rows, HEAD_DIM), per_b),
        scratch_shapes=[pltpu.VMEM((rows, 1), jnp.float32),
                        pltpu.VMEM((rows, 1), jnp.float32),
                        pltpu.VMEM((rows, HEAD_DIM), jnp.float32),
                        pltpu.VMEM((1, LANE), jnp.float32),
                        pltpu.VMEM((npp, width), jnp.float32)],
    )
    return pl.pallas_call(
        kern,
        grid_spec=grid_spec,
        out_shape=jax.ShapeDtypeStruct((nb, rows, HEAD_DIM), jnp.float32),
        compiler_params=_params(("arbitrary", "arbitrary")),
        name="decode_attention",
    )(page_table, q16, k_new16, v_new16, lf_row, *([kc] * npp), *([vc] * npp), *([lfc] * npp))


def _outproj_kernel(c_ref, a_ref, x_ref, w1_ref, w2_ref, g_ref, h_ref, hn_ref):
    h = x_ref[...] + _dot(c_ref[...], w1_ref[...]) + _dot(a_ref[...], w2_ref[...])
    h_ref[...] = h
    hn_ref[...] = _bf16(_rms(h, g_ref[...]))


def _outproj(conv_out, attn_out, x, w_out, g, *, tm):
    m, d = x.shape
    half = conv_out.shape[1]
    return pl.pallas_call(
        _outproj_kernel,
        grid=(m // tm,),
        in_specs=[
            pl.BlockSpec((tm, half), lambda i: (i, 0)),
            pl.BlockSpec((tm, half), lambda i: (i, 0)),
            pl.BlockSpec((tm, d), lambda i: (i, 0)),
            pl.BlockSpec((half, d), lambda i: (0, 0)),
            pl.BlockSpec((half, d), lambda i: (1, 0)),
            pl.BlockSpec((1, d), lambda i: (0, 0)),
        ],
        out_specs=[
            pl.BlockSpec((tm, d), lambda i: (i, 0)),
            pl.BlockSpec((tm, d), lambda i: (i, 0)),
        ],
        out_shape=[
            jax.ShapeDtypeStruct((m, d), jnp.float32),
            jax.ShapeDtypeStruct((m, d), jnp.bfloat16),
        ],
        compiler_params=_params(("parallel",)),
        name="outproj",
    )(conv_out, attn_out, x, w_out, w_out, g)


def _ffn_prompt_kernel(hn_ref, h_ref, wa_ref, wb_ref, wd_ref, cwa_ref, cwb_ref,
                       cba_ref, cbb_ref, gf_ref,
                       y_ref, ta_ref, tb_ref, acc_scr, up_scr, tail_scr, *, tm, tf):
    i = pl.program_id(0)
    j = pl.program_id(1)
    nj = pl.num_programs(1)
    hist = SUBLANE
    slot = pl.multiple_of(j * hist, hist)

    @pl.when(j == 0)
    def _():
        acc_scr[...] = jnp.zeros_like(acc_scr)

    @pl.when(i == 0)
    def _():
        up_scr[0:hist, :] = jnp.zeros((hist, 2 * tf), jnp.float32)

    @pl.when(i > 0)
    def _():
        up_scr[0:hist, :] = tail_scr[pl.ds(slot, hist), :]

    hn = hn_ref[...]
    up_scr[hist:hist + tm, 0:tf] = _dot(hn, wa_ref[...])
    up_scr[hist:hist + tm, tf:2 * tf] = _dot(hn, wb_ref[...])

    def conv(cols, cw_ref, cb_ref):
        return (cb_ref[...]
                + cw_ref[0:1, :] * up_scr[hist - 2:hist - 2 + tm, cols]
                + cw_ref[1:2, :] * up_scr[hist - 1:hist - 1 + tm, cols]
                + cw_ref[2:3, :] * up_scr[hist:hist + tm, cols])

    ca = conv(slice(0, tf), cwa_ref, cba_ref)
    cb = conv(slice(tf, 2 * tf), cwb_ref, cbb_ref)
    act = _bf16(ca * _sigmoid(ca) * cb)
    acc_scr[...] += _dot(act, wd_ref[...])

    tail = up_scr[tm:tm + hist, :]
    tail_scr[pl.ds(slot, hist), :] = tail

    @pl.when(i == pl.num_programs(0) - 1)
    def _():
        cols = pl.ds(pl.multiple_of(j * tf, tf), tf)
        ta_ref[:, cols] = tail[:, 0:tf]
        tb_ref[:, cols] = tail[:, tf:2 * tf]

    @pl.when(j == nj - 1)
    def _():
        y_ref[...] = _rms(h_ref[...] + acc_scr[...], gf_ref[...])


def _ffn_prompt(hn, h, w_up, w_down, cw, cb, gf, *, tm, tf):
    m, d = h.shape
    dff = w_down.shape[0]
    assert dff % tf == 0 and m % tm == 0
    nj = dff // tf
    kern = functools.partial(_ffn_prompt_kernel, tm=tm, tf=tf)
    return pl.pallas_call(
        kern,
        grid=(m // tm, nj),
        in_specs=[
            pl.BlockSpec((tm, d), lambda i, j: (i, 0)),
            pl.BlockSpec((tm, d), lambda i, j: (i, 0)),
            pl.BlockSpec((d, tf), lambda i, j: (0, j)),
            pl.BlockSpec((d, tf), lambda i, j: (0, j + nj)),
            pl.BlockSpec((tf, d), lambda i, j: (j, 0)),
            pl.BlockSpec((3, tf), lambda i, j: (0, j)),
            pl.BlockSpec((3, tf), lambda i, j: (0, j + nj)),
            pl.BlockSpec((1, tf), lambda i, j: (0, j)),
            pl.BlockSpec((1, tf), lambda i, j: (0, j + nj)),
            pl.BlockSpec((1, d), lambda i, j: (0, 0)),
        ],
        out_specs=[
            pl.BlockSpec((tm, d), lambda i, j: (i, 0)),
            pl.BlockSpec((SUBLANE, dff), lambda i, j: (0, 0)),
            pl.BlockSpec((SUBLANE, dff), lambda i, j: (0, 0)),
        ],
        out_shape=[
            jax.ShapeDtypeStruct((m, d), jnp.float32),
            jax.ShapeDtypeStruct((SUBLANE, dff), jnp.float32),
            jax.ShapeDtypeStruct((SUBLANE, dff), jnp.float32),
        ],
        scratch_shapes=[pltpu.VMEM((tm, d), jnp.float32),
                        pltpu.VMEM((SUBLANE + tm, 2 * tf), jnp.float32),
                        pltpu.VMEM((nj * SUBLANE, 2 * tf), jnp.float32)],
        compiler_params=_params(("arbitrary", "arbitrary")),
        name="ffn_prompt",
    )(hn, h, w_up, w_up, w_down, cw, cw, cb, cb, gf)


def _ffn_sample_kernel(hn_ref, h_ref, wa_ref, wb_ref, wd_ref, sta_ref, stb_ref,
                       cwa_ref, cwb_ref, cba_ref, cbb_ref, gf_ref,
                       y_ref, ua_ref, ub_ref, acc_scr):
    j = pl.program_id(0)

    @pl.when(j == 0)
    def _():
        acc_scr[...] = jnp.zeros_like(acc_scr)

    hn = hn_ref[...]
    ua = _dot(hn, wa_ref[...])
    ub = _dot(hn, wb_ref[...])
    ua_ref[...] = ua
    ub_ref[...] = ub
    ca = cba_ref[...] + cwa_ref[0:1, :] * sta_ref[0] + cwa_ref[1:2, :] * sta_ref[1] + cwa_ref[2:3, :] * ua
    cb = cbb_ref[...] + cwb_ref[0:1, :] * stb_ref[0] + cwb_ref[1:2, :] * stb_ref[1] + cwb_ref[2:3, :] * ub
    act = _bf16(ca * _sigmoid(ca) * cb)
    acc_scr[...] += _dot(act, wd_ref[...])

    @pl.when(j == pl.num_programs(0) - 1)
    def _():
        y_ref[...] = _rms(h_ref[...] + acc_scr[...], gf_ref[...])


def _ffn_sample(hn, h, w_up, w_down, state_t, cw, cb, gf, *, tf):
    m, d = h.shape
    dff = w_down.shape[0]
    nj = dff // tf
    return pl.pallas_call(
        _ffn_sample_kernel,
        grid=(nj,),
        in_specs=[
            pl.BlockSpec((m, d), lambda j: (0, 0)),
            pl.BlockSpec((m, d), lambda j: (0, 0)),
            pl.BlockSpec((d, tf), lambda j: (0, j)),
            pl.BlockSpec((d, tf), lambda j: (0, j + nj)),
            pl.BlockSpec((tf, d), lambda j: (j, 0)),
            pl.BlockSpec((2, m, tf), lambda j: (0, 0, j)),
            pl.BlockSpec((2, m, tf), lambda j: (0, 0, j + nj)),
            pl.BlockSpec((3, tf), lambda j: (0, j)),
            pl.BlockSpec((3, tf), lambda j: (0, j + nj)),
            pl.BlockSpec((1, tf), lambda j: (0, j)),
            pl.BlockSpec((1, tf), lambda j: (0, j + nj)),
            pl.BlockSpec((1, d), lambda j: (0, 0)),
        ],
        out_specs=[
            pl.BlockSpec((m, d), lambda j: (0, 0)),
            pl.BlockSpec((m, tf), lambda j: (0, j)),
            pl.BlockSpec((m, tf), lambda j: (0, j)),
        ],
        out_shape=[
            jax.ShapeDtypeStruct((m, d), jnp.float32),
            jax.ShapeDtypeStruct((m, dff), jnp.float32),
            jax.ShapeDtypeStruct((m, dff), jnp.float32),
        ],
        scratch_shapes=[pltpu.VMEM((m, d), jnp.float32)],
        compiler_params=_params(("arbitrary",)),
        name="ffn_sample",
    )(hn, h, w_up, w_up, w_down, state_t, state_t, cw, cw, cb, cb, gf)


def _pick(n, prefs):
    for p in prefs:
        if n % p == 0:
            return p
    return n


def kernel(x_prompt, x_sample, cache_k, cache_v, cache_logf, state_conv, state_ffn, page_table,
           norm_mix_g, w_in, b_f, conv_dw_w, conv_dw_b, conv_ln_g, conv_ln_b, w_out,
           norm_ffn_g, w_up, ffn_dw_w, ffn_dw_b, w_down, norm_final_g):
    f32 = jnp.float32
    depth = w_in.shape[0]
    assert depth == 1 and x_prompt.shape[0] == 1 and x_sample.shape[1] == 1
    s, d = x_prompt.shape[1], x_prompt.shape[2]
    nb = x_sample.shape[0]
    n_pool, page = cache_k.shape[1], cache_k.shape[2]
    assert cache_k.shape[3:] == (N_HEADS, HEAD_DIM)
    dff = w_down.shape[1]
    conv_ch = conv_dw_w.shape[2]
    assert conv_ch == ATTN_WIDTH
    n_main = 2 * conv_ch + 3 * ATTN_WIDTH

    w_in_bf = _bf16(w_in[0])
    w_main = w_in_bf[:, :n_main]
    q0, v0 = 2 * conv_ch, 2 * conv_ch + 2 * ATTN_WIDTH
    wt_qv = jnp.transpose(jnp.concatenate(
        [w_main[:, q0:q0 + ATTN_WIDTH], w_main[:, v0:v0 + ATTN_WIDTH]], axis=1))
    wf_pad = jnp.pad(w_in_bf[:, n_main:], ((0, 0), (0, LANE - N_HEADS)))
    bf_row = jnp.pad(b_f[0].reshape(1, N_HEADS), ((0, 0), (0, LANE - N_HEADS)))
    w_out_bf = _bf16(w_out[0])
    w_up_bf = _bf16(w_up[0])
    w_down_bf = _bf16(w_down[0])
    row = lambda v: v.reshape(1, -1)
    g_mix, g_ffn, g_fin = row(norm_mix_g[0]), row(norm_ffn_g[0]), row(norm_final_g)
    cw, cb_, lng, lnb = conv_dw_w[0], row(conv_dw_b[0]), row(conv_ln_g[0]), row(conv_ln_b[0])
    fcw, fcb = ffn_dw_w[0], row(ffn_dw_b[0])
    kw = cw.shape[0]
    tril = _bf16(jnp.tril(jnp.ones((SCAN_CHUNK, SCAN_CHUNK), f32)))
    src = jnp.arange(3 * LANE)
    dst = jnp.arange(ATTN_WIDTH)
    psel = _bf16(((src[:, None] % LANE == dst[None, :] // LANE)
                  & (src[:, None] // LANE == dst[None, :] % LANE)
                  & (src[:, None] % LANE < N_HEADS)).astype(f32))

    tf = _pick(dff, (512, 256, 128))

    xp = x_prompt[0]
    ag_p, kv_p, qat_p, ka_p, vat_p, lf_p = _inproj_prompt(
        xp, g_mix, w_main, wt_qv, wf_pad, bf_row, tril, psel,
        tm=_pick(s, (512, 256)), tn=512)
    conv_p, utail_p = _conv_prompt(ag_p, cw, cb_, lng, lnb, tb=_pick(s, (256, 128)))
    attn_p = _flash_prompt(qat_p, ka_p, vat_p, tq=_pick(s, (512, 256, 128)))
    h_p, hn_p = _outproj(conv_p, attn_p, xp, w_out_bf, g_ffn, tm=_pick(s, (512, 256)))
    y_p, ta_p, tb_p = _ffn_prompt(hn_p, h_p, w_up_bf, w_down_bf, fcw, fcb, g_fin,
                                  tm=_pick(s, (512, 256)), tf=tf)

    y_prompt = y_p[None]
    k_prompt = kv_p[:, :ATTN_WIDTH].reshape(1, 1, s, N_HEADS, HEAD_DIM)
    v_prompt = kv_p[:, ATTN_WIDTH:].reshape(1, 1, s, N_HEADS, HEAD_DIM)
    logf_prompt = lf_p[:, :N_HEADS].reshape(1, 1, s, N_HEADS)
    conv_prompt = utail_p[utail_p.shape[0] - (kw - 1):][None, None]
    ffn_prompt = jnp.concatenate([ta_p[SUBLANE - 2:], tb_p[SUBLANE - 2:]], axis=1)[None, None]

    xs = x_sample[:, 0, :]
    ag_s, kv_s, q_s, lf_s128 = _inproj_sample(xs, g_mix, w_main, wf_pad, bf_row)
    conv_s, u_s = _conv_sample(ag_s, jnp.swapaxes(state_conv[0], 0, 1), cw, cb_, lng, lnb)

    pad_heads = lambda a: jnp.pad(a.reshape(nb, N_HEADS, HEAD_DIM),
                                  ((0, 0), (0, HEAD_ROWS - N_HEADS), (0, 0)))
    q16 = pad_heads(q_s)
    kn16 = pad_heads(kv_s[:, :ATTN_WIDTH])
    vn16 = pad_heads(kv_s[:, ATTN_WIDTH:])
    lf_s = lf_s128[:, :N_HEADS]
    lf_row = jnp.tile(lf_s, (1, LANE // N_HEADS)).reshape(nb, 1, LANE)
    kc = cache_k[0].reshape(n_pool, page * N_HEADS, HEAD_DIM)
    vc = cache_v[0].reshape(n_pool, page * N_HEADS, HEAD_DIM)
    lfc = cache_logf[0].astype(f32).reshape(n_pool, 1, page * N_HEADS)
    attn_s16 = _decode_attention(page_table, q16, kn16, vn16, lf_row, kc, vc, lfc)
    attn_s = _bf16(attn_s16[:, :N_HEADS, :].reshape(nb, ATTN_WIDTH))

    h_s, hn_s = _outproj(conv_s, attn_s, xs, w_out_bf, g_ffn, tm=nb)
    y_s, ua_s, ub_s = _ffn_sample(hn_s, h_s, w_up_bf, w_down_bf,
                                  jnp.swapaxes(state_ffn[0], 0, 1), fcw, fcb, g_fin, tf=tf)

    y_sample = y_s[:, None, :]
    k_sample = kv_s[:, :ATTN_WIDTH].reshape(1, nb, 1, N_HEADS, HEAD_DIM)
    v_sample = kv_s[:, ATTN_WIDTH:].reshape(1, nb, 1, N_HEADS, HEAD_DIM)
    logf_sample = lf_s.reshape(1, nb, 1, N_HEADS)
    conv_sample = jnp.concatenate([state_conv[0][:, 1:, :], u_s[:, None, :]], axis=1)[None]
    up_s = jnp.concatenate([ua_s, ub_s], axis=1)
    ffn_sample = jnp.concatenate([state_ffn[0][:, 1:, :], up_s[:, None, :]], axis=1)[None]

    return (y_prompt, y_sample, k_prompt, v_prompt, logf_prompt, conv_prompt, ffn_prompt,
            k_sample, v_sample, logf_sample, conv_sample, ffn_sample)
```

```python
import functools
import math

import jax
import jax.numpy as jnp
from jax import lax
from jax.experimental import pallas as pl
from jax.experimental.pallas import tpu as pltpu

EPS = 1e-6
LANE = 128
SUBLANE = 8
MXU_DIM = 256
HEAD_DIM = 128
N_HEADS = 8
ATTN_WIDTH = N_HEADS * HEAD_DIM
HEAD_ROWS = 16
QK_DEPTH = MXU_DIM
V_ROWS = HEAD_DIM + 16
SCAN_CHUNK = 256
PAGES_PER_STEP = 8
VMEM_LIMIT = 56 * 1024 * 1024
LOG2E = math.log2(math.e)
NEG = -1e30

_NT = (((1,), (1,)), ((), ()))


def _bf16(x):
    return x.astype(jnp.bfloat16)


def _dot(a, b):
    return jnp.dot(a, b, preferred_element_type=jnp.float32)


def _split3(x):
    hi = _bf16(x)
    r1 = x - hi.astype(jnp.float32)
    mid = _bf16(r1)
    lo = _bf16(r1 - mid.astype(jnp.float32))
    return hi, mid, lo


def _log_sigmoid(x):
    return jnp.minimum(x, 0.0) - jnp.log1p(jnp.exp(-jnp.abs(x)))


def _sigmoid(x):
    return 1.0 / (1.0 + jnp.exp(-x))


def _rms(x, g):
    ms = jnp.mean(x * x, axis=-1, keepdims=True)
    return x * lax.rsqrt(ms + EPS) * g


def _params(sem):
    return pltpu.CompilerParams(dimension_semantics=sem, vmem_limit_bytes=VMEM_LIMIT)


def _inproj_prompt_kernel(x_ref, g_ref, w_ref, wt_ref, wf_ref, bf_ref, tril_ref, psel_ref,
                          ag_ref, kv_ref, qat_ref, ka_ref, vat_ref, lf_ref,
                          xn_scr, aug_scr, carry_scr, *, tm, tn, q_scale):
    i = pl.program_id(0)
    j = pl.program_id(1)
    hpt = tn // HEAD_DIM
    tps = ATTN_WIDTH // tn

    @pl.when(j == 0)
    def _():
        xn_scr[...] = _bf16(_rms(x_ref[...], g_ref[...]))
        lf = _log_sigmoid(_dot(xn_scr[...], wf_ref[...]) + bf_ref[...])
        lf_ref[...] = lf

        @pl.when(i == 0)
        def _():
            carry_scr[...] = jnp.zeros_like(carry_scr)

        carry = carry_scr[0:1, :]
        tril = tril_ref[...]
        for c in range(tm // SCAN_CHUNK):
            rows = slice(c * SCAN_CHUNK, (c + 1) * SCAN_CHUNK)
            hi, mid, lo = _split3(lf[rows, :])
            cs = _dot(tril, hi) + _dot(tril, mid) + _dot(tril, lo) + carry
            h2, m2, l2 = _split3(cs * (-LOG2E))
            aug = (_dot(h2, psel_ref[0:LANE, :]) + _dot(m2, psel_ref[LANE:2 * LANE, :])
                   + _dot(l2, psel_ref[2 * LANE:3 * LANE, :]))
            aug_scr[rows, :] = _bf16(aug)
            carry = cs[SCAN_CHUNK - 1:SCAN_CHUNK, :]
        carry_scr[...] = jnp.broadcast_to(carry, carry_scr.shape)

    def nn():
        return _dot(xn_scr[...], w_ref[...])

    def nt():
        return lax.dot_general(wt_ref[...], xn_scr[...], _NT,
                               preferred_element_type=jnp.float32)

    @pl.when(j < 2 * tps)
    def _():
        ag_ref[...] = nn()

    for t in range(tps):
        @pl.when(j == 2 * tps + t)
        def _(t=t):
            zt = nt() * q_scale
            ones = (lax.broadcasted_iota(jnp.int32, (QK_DEPTH - HEAD_DIM, tm), 0) < 3
                    ).astype(jnp.bfloat16)
            for hh in range(hpt):
                r = (t * hpt + hh) * QK_DEPTH
                qat_ref[r:r + HEAD_DIM, :] = _bf16(zt[hh * HEAD_DIM:(hh + 1) * HEAD_DIM, :])
                qat_ref[r + HEAD_DIM:r + QK_DEPTH, :] = ones

        @pl.when(j == 3 * tps + t)
        def _(t=t):
            z = nn()
            kv_ref[...] = z
            for hh in range(hpt):
                h = t * hpt + hh
                c = h * QK_DEPTH
                ka_ref[:, c:c + HEAD_DIM] = _bf16(z[:, hh * HEAD_DIM:(hh + 1) * HEAD_DIM])
                ka_ref[:, c + HEAD_DIM:c + QK_DEPTH] = aug_scr[:, h * LANE:(h + 1) * LANE]

        @pl.when(j == 4 * tps + t)
        def _(t=t):
            kv_ref[...] = nn()
            zt = nt()
            ones = (lax.broadcasted_iota(jnp.int32, (V_ROWS - HEAD_DIM, tm), 0) == 0
                    ).astype(jnp.bfloat16)
            for hh in range(hpt):
                r = (t * hpt + hh) * V_ROWS
                vat_ref[r:r + HEAD_DIM, :] = _bf16(zt[hh * HEAD_DIM:(hh + 1) * HEAD_DIM, :])
                vat_ref[r + HEAD_DIM:r + V_ROWS, :] = ones


def _inproj_prompt(x, g, w_tiles, wt_qv, wf_pad, bf_row, tril, psel, *, tm):
    m, d = x.shape
    nj, _, tn = w_tiles.shape
    tps = ATTN_WIDTH // tn
    assert nj == 5 * tps and m % tm == 0 and tm % SCAN_CHUNK == 0
    kern = functools.partial(_inproj_prompt_kernel, tm=tm, tn=tn,
                             q_scale=HEAD_DIM ** -0.5 * LOG2E)

    def w_map(i, j):
        return (jnp.where((j >= 2 * tps) & (j < 3 * tps), 2 * tps - 1, j), 0, 0)

    def wt_map(i, j):
        t = jnp.clip(j - 2 * tps, 0, tps - 1) + jnp.where(j >= 3 * tps, 1, 0) \
            + jnp.clip(j - 4 * tps, 0, tps - 1)
        return (jnp.minimum(t, 2 * tps - 1), 0)

    return pl.pallas_call(
        kern,
        grid=(m // tm, nj),
        in_specs=[
            pl.BlockSpec((tm, d), lambda i, j: (i, 0)),
            pl.BlockSpec((1, d), lambda i, j: (0, 0)),
            pl.BlockSpec((None, d, tn), w_map),
            pl.BlockSpec((tn, d), wt_map),
            pl.BlockSpec((d, LANE), lambda i, j: (0, 0)),
            pl.BlockSpec((1, LANE), lambda i, j: (0, 0)),
            pl.BlockSpec((SCAN_CHUNK, SCAN_CHUNK), lambda i, j: (0, 0)),
            pl.BlockSpec((3 * LANE, ATTN_WIDTH), lambda i, j: (0, 0)),
        ],
        out_specs=[
            pl.BlockSpec((tm, tn), lambda i, j: (i, jnp.minimum(j, 2 * tps - 1))),
            pl.BlockSpec((tm, tn), lambda i, j: (i, jnp.clip(j - 3 * tps, 0, 2 * tps - 1))),
            pl.BlockSpec((None, N_HEADS * QK_DEPTH, tm), lambda i, j: (i, 0, 0)),
            pl.BlockSpec((tm, N_HEADS * QK_DEPTH), lambda i, j: (i, 0)),
            pl.BlockSpec((None, N_HEADS * V_ROWS, tm), lambda i, j: (i, 0, 0)),
            pl.BlockSpec((tm, LANE), lambda i, j: (i, 0)),
        ],
        out_shape=[
            jax.ShapeDtypeStruct((m, 2 * ATTN_WIDTH), jnp.float32),
            jax.ShapeDtypeStruct((m, 2 * ATTN_WIDTH), jnp.float32),
            jax.ShapeDtypeStruct((m // tm, N_HEADS * QK_DEPTH, tm), jnp.bfloat16),
            jax.ShapeDtypeStruct((m, N_HEADS * QK_DEPTH), jnp.bfloat16),
            jax.ShapeDtypeStruct((m // tm, N_HEADS * V_ROWS, tm), jnp.bfloat16),
            jax.ShapeDtypeStruct((m, LANE), jnp.float32),
        ],
        scratch_shapes=[pltpu.VMEM((tm, d), jnp.bfloat16),
                        pltpu.VMEM((tm, ATTN_WIDTH), jnp.bfloat16),
                        pltpu.VMEM((SUBLANE, LANE), jnp.float32)],
        compiler_params=_params(("arbitrary", "arbitrary")),
        name="inproj_prompt",
    )(x, g, w_tiles, wt_qv, wf_pad, bf_row, tril, psel)


def _inproj_sample_kernel(x_ref, g_ref, w_ref, wf_ref, bf_ref,
                          ag_ref, kv_ref, q_ref, lf_ref, xn_scr, *, q_scale):
    j = pl.program_id(0)

    @pl.when(j == 0)
    def _():
        xn_scr[...] = _bf16(_rms(x_ref[...], g_ref[...]))
        lf_ref[...] = _log_sigmoid(_dot(xn_scr[...], wf_ref[...]) + bf_ref[...])

    z = _dot(xn_scr[...], w_ref[...])

    @pl.when(j < 2)
    def _():
        ag_ref[...] = z

    @pl.when(j == 2)
    def _():
        q_ref[...] = _bf16(z * q_scale)

    @pl.when(j > 2)
    def _():
        kv_ref[...] = z


def _inproj_sample(x, g, w_main, wf_pad, bf_row):
    m, d = x.shape
    tn = ATTN_WIDTH
    kern = functools.partial(_inproj_sample_kernel, q_scale=HEAD_DIM ** -0.5)
    return pl.pallas_call(
        kern,
        grid=(5,),
        in_specs=[
            pl.BlockSpec((m, d), lambda j: (0, 0)),
            pl.BlockSpec((1, d), lambda j: (0, 0)),
            pl.BlockSpec((d, tn), lambda j: (0, j)),
            pl.BlockSpec((d, LANE), lambda j: (0, 0)),
            pl.BlockSpec((1, LANE), lambda j: (0, 0)),
        ],
        out_specs=[
            pl.BlockSpec((m, tn), lambda j: (0, jnp.minimum(j, 1))),
            pl.BlockSpec((m, tn), lambda j: (0, jnp.clip(j - 3, 0, 1))),
            pl.BlockSpec((m, tn), lambda j: (0, 0)),
            pl.BlockSpec((m, LANE), lambda j: (0, 0)),
        ],
        out_shape=[
            jax.ShapeDtypeStruct((m, 2 * tn), jnp.float32),
            jax.ShapeDtypeStruct((m, 2 * tn), jnp.float32),
            jax.ShapeDtypeStruct((m, tn), jnp.bfloat16),
            jax.ShapeDtypeStruct((m, LANE), jnp.float32),
        ],
        scratch_shapes=[pltpu.VMEM((m, d), jnp.bfloat16)],
        compiler_params=_params(("arbitrary",)),
        name="inproj_sample",
    )(x, g, w_main, wf_pad, bf_row)


def _layernorm_swish(y, g, b):
    mu = jnp.mean(y, axis=-1, keepdims=True)
    yc = y - mu
    var = jnp.mean(yc * yc, axis=-1, keepdims=True)
    yn = yc * lax.rsqrt(var + EPS) * g + b
    return yn * _sigmoid(yn)


def _conv_prompt_kernel(a_ref, gate_ref, w_ref, b_ref, lng_ref, lnb_ref,
                        out_ref, tail_ref, uext_scr, ush_scr, y_scr, *, tb, kw, hist):
    i = pl.program_id(0)

    @pl.when(i == 0)
    def _():
        uext_scr[0:hist, :] = jnp.zeros((hist, uext_scr.shape[1]), jnp.float32)

    uext_scr[hist:hist + tb, :] = a_ref[...] * _sigmoid(gate_ref[...])
    span = ush_scr.shape[1]
    for sh in range(1, SUBLANE):
        ush_scr[sh - 1] = uext_scr[sh:sh + span, :]
    ch = a_ref.shape[1]
    rows = 128
    off = hist - (kw - 1)
    nrc = tb // rows
    for cc in range(ch // LANE):
        cs = slice(cc * LANE, (cc + 1) * LANE)
        accs = [jnp.broadcast_to(b_ref[:, cs], (rows, LANE))] * nrc
        for t in range(kw):
            sh = (off + t) % SUBLANE
            wt = w_ref[t:t + 1, cs]
            for rc in range(nrc):
                r0 = rc * rows + (off + t) - sh
                if sh == 0:
                    u = uext_scr[r0:r0 + rows, cs]
                else:
                    u = ush_scr[sh - 1, r0:r0 + rows, cs]
                accs[rc] = accs[rc] + wt * u
        for rc in range(nrc):
            y_scr[rc * rows:(rc + 1) * rows, cs] = accs[rc]
    out_ref[...] = _bf16(_layernorm_swish(y_scr[...], lng_ref[...], lnb_ref[...]))
    tail = uext_scr[tb:tb + hist, :]
    uext_scr[0:hist, :] = tail
    tail_ref[...] = tail


def _conv_prompt(ag, w, b, lng, lnb, *, tb):
    s = ag.shape[0]
    ch = ag.shape[1] // 2
    kw = w.shape[0]
    hist = 32
    assert kw - 1 <= hist and s % tb == 0
    kern = functools.partial(_conv_prompt_kernel, tb=tb, kw=kw, hist=hist)
    return pl.pallas_call(
        kern,
        grid=(s // tb,),
        in_specs=[
            pl.BlockSpec((tb, ch), lambda i: (i, 0)),
            pl.BlockSpec((tb, ch), lambda i: (i, 1)),
            pl.BlockSpec((kw, ch), lambda i: (0, 0)),
            pl.BlockSpec((1, ch), lambda i: (0, 0)),
            pl.BlockSpec((1, ch), lambda i: (0, 0)),
            pl.BlockSpec((1, ch), lambda i: (0, 0)),
        ],
        out_specs=[
            pl.BlockSpec((tb, ch), lambda i: (i, 0)),
            pl.BlockSpec((hist, ch), lambda i: (0, 0)),
        ],
        out_shape=[
            jax.ShapeDtypeStruct((s, ch), jnp.bfloat16),
            jax.ShapeDtypeStruct((hist, ch), jnp.float32),
        ],
        scratch_shapes=[pltpu.VMEM((hist + tb, ch), jnp.float32),
                        pltpu.VMEM((SUBLANE - 1, hist + tb - SUBLANE, ch), jnp.float32),
                        pltpu.VMEM((tb, ch), jnp.float32)],
        compiler_params=_params(("arbitrary",)),
        name="conv_prompt",
    )(ag, ag, w, b, lng, lnb)


def _conv_sample_kernel(a_ref, gate_ref, st_ref, w_ref, b_ref, lng_ref, lnb_ref,
                        out_ref, u_ref, *, kw):
    u = a_ref[...] * _sigmoid(gate_ref[...])
    u_ref[...] = u
    acc = b_ref[...] + w_ref[kw - 1:kw, :] * u
    for t in range(kw - 1):
        acc = acc + w_ref[t:t + 1, :] * st_ref[t]
    out_ref[...] = _bf16(_layernorm_swish(acc, lng_ref[...], lnb_ref[...]))


def _conv_sample(ag, state_t, w, b, lng, lnb):
    nb = ag.shape[0]
    ch = ag.shape[1] // 2
    kw = w.shape[0]
    kern = functools.partial(_conv_sample_kernel, kw=kw)
    return pl.pallas_call(
        kern,
        grid=(1,),
        in_specs=[
            pl.BlockSpec((nb, ch), lambda i: (0, 0)),
            pl.BlockSpec((nb, ch), lambda i: (0, 1)),
            pl.BlockSpec((kw - 1, nb, ch), lambda i: (0, 0, 0)),
            pl.BlockSpec((kw, ch), lambda i: (0, 0)),
            pl.BlockSpec((1, ch), lambda i: (0, 0)),
            pl.BlockSpec((1, ch), lambda i: (0, 0)),
            pl.BlockSpec((1, ch), lambda i: (0, 0)),
        ],
        out_specs=[
            pl.BlockSpec((nb, ch), lambda i: (0, 0)),
            pl.BlockSpec((nb, ch), lambda i: (0, 0)),
        ],
        out_shape=[
            jax.ShapeDtypeStruct((nb, ch), jnp.bfloat16),
            jax.ShapeDtypeStruct((nb, ch), jnp.float32),
        ],
        compiler_params=_params(("arbitrary",)),
        name="conv_sample",
    )(ag, ag, state_t, w, b, lng, lnb)


def _flash_kernel(qat_ref, ka_ref, vat_ref, o_ref, s_scr, acc_scr, *, tk):
    i = pl.program_id(1)
    tq = 2 * tk
    qat = jnp.concatenate([qat_ref[0], qat_ref[1]], axis=1)
    acc_scr[...] = jnp.zeros_like(acc_scr)

    def block_of(k):
        return jnp.where(k < 2, 2 * i + k, k - 2)

    def scores(k, slot, diag=None):
        j = k - 2 if diag is None else 2 * i + diag
        r0 = pl.multiple_of(j * tk, tk)
        st = _dot(ka_ref[pl.ds(r0, tk), :], qat)
        if diag is not None:
            key = lax.broadcasted_iota(jnp.int32, (tk, tq), 0) + diag * tk
            qry = lax.broadcasted_iota(jnp.int32, (tk, tq), 1)
            st = jnp.where(key <= qry, st, NEG)
        s_scr[slot] = st
        return jnp.max(st, axis=0, keepdims=True)

    def accumulate(k, slot, m_old, mx):
        m_new = jnp.maximum(m_old, mx)
        alpha = jnp.exp2(m_old - m_new)
        pt = _bf16(jnp.exp2(s_scr[slot] - m_new))
        acc_scr[...] = alpha * acc_scr[...] + _dot(vat_ref[block_of(k)], pt)
        return m_new

    def body(p, carry):
        m, mx0 = carry
        k = 2 * p
        mx1 = scores(k + 1, 1)
        m = accumulate(k, 0, m, mx0)
        mx0 = scores(k + 2, 0)
        m = accumulate(k + 1, 1, m, mx1)
        return m, mx0

    m0 = jnp.full((1, tq), NEG, jnp.float32)
    last = 2 * i + 1

    @pl.when(i == 0)
    def _():
        mx0 = scores(0, 0, diag=0)
        mx1 = scores(1, 1, diag=1)
        m = accumulate(0, 0, m0, mx0)
        accumulate(1, 1, m, mx1)

    @pl.when(i > 0)
    def _():
        mx0 = scores(0, 0, diag=0)
        mx1 = scores(1, 1, diag=1)
        m = accumulate(0, 0, m0, mx0)
        mx0 = scores(2, 0)
        m = accumulate(1, 1, m, mx1)
        m, mx0 = lax.fori_loop(1, i, body, (m, mx0))
        mx1 = scores(last, 1)
        m = accumulate(last - 1, 0, m, mx0)
        accumulate(last, 1, m, mx1)

    acc = acc_scr[...]
    out_t = acc[0:HEAD_DIM, :] / acc[HEAD_DIM:HEAD_DIM + 1, :]
    o_ref[...] = _bf16(jnp.transpose(out_t))


def _flash_prompt(qat, ka, vat):
    s = ka.shape[0]
    nblk, _, tk = qat.shape
    tq = 2 * tk
    assert nblk * tk == s and s % tq == 0 and vat.shape == (nblk, N_HEADS * V_ROWS, tk)
    kern = functools.partial(_flash_kernel, tk=tk)
    return pl.pallas_call(
        kern,
        grid=(N_HEADS, s // tq),
        in_specs=[
            pl.BlockSpec((2, QK_DEPTH, tk), lambda h, i: (i, h, 0)),
            pl.BlockSpec((s, QK_DEPTH), lambda h, i: (0, h)),
            pl.BlockSpec((nblk, V_ROWS, tk), lambda h, i: (0, h, 0)),
        ],
        out_specs=pl.BlockSpec((tq, HEAD_DIM), lambda h, i: (i, h)),
        out_shape=jax.ShapeDtypeStruct((s, ATTN_WIDTH), jnp.bfloat16),
        scratch_shapes=[pltpu.VMEM((2, tk, tq), jnp.float32),
                        pltpu.VMEM((V_ROWS, tq), jnp.float32)],
        compiler_params=_params(("parallel", "arbitrary")),
        name="flash_prompt",
    )(qat, ka, vat)


def _decode_kernel(pt_ref, q_ref, kn_ref, vn_ref, lfn_ref, *refs, npp):
    k_refs = refs[0:npp]
    v_refs = refs[npp:2 * npp]
    lf_refs = refs[2 * npp:3 * npp]
    o_ref = refs[3 * npp]
    m_scr, l_scr, acc_scr, run_scr, lf_scr = refs[3 * npp + 1:]
    g = pl.program_id(1)
    rows = q_ref.shape[0]
    width = k_refs[0].shape[0]
    nch = width // LANE
    q = q_ref[...]

    @pl.when(g == 0)
    def _():
        kn = _bf16(kn_ref[...]).astype(jnp.float32)
        s_new = jnp.sum(q.astype(jnp.float32) * kn, axis=-1, keepdims=True)
        m_scr[...] = s_new
        l_scr[...] = jnp.ones_like(l_scr)
        acc_scr[...] = _bf16(vn_ref[...]).astype(jnp.float32)
        run_scr[...] = lfn_ref[...]

    for p in range(npp):
        lf_scr[p:p + 1, :] = lf_refs[p][...]
    x = lf_scr[...]
    xs = jnp.concatenate([x[:, c * LANE:(c + 1) * LANE] for c in range(nch)], axis=0)
    lane = lax.broadcasted_iota(jnp.int32, xs.shape, 1)
    y = xs
    tot = xs
    for sh in (8, 16, 32, 64):
        y = y + jnp.where(lane + sh < LANE, pltpu.roll(y, LANE - sh, axis=1), 0.0)
        tot = tot + pltpu.roll(tot, sh, axis=1)
    excl = y - xs
    later = jnp.zeros((npp, LANE), jnp.float32)
    pieces = [None] * nch
    for c in reversed(range(nch)):
        pieces[c] = excl[c * npp:(c + 1) * npp, :] + later
        later = later + tot[c * npp:(c + 1) * npp, :]
    within = jnp.concatenate(pieces, axis=1)
    page_tot = later

    hrow = lax.broadcasted_iota(jnp.int32, (rows, width), 0)
    hcol = lax.broadcasted_iota(jnp.int32, (rows, width), 1) & (N_HEADS - 1)
    own = hrow == hcol
    run = run_scr[...]
    scores = [None] * npp
    for p in reversed(range(npp)):
        bias = within[p:p + 1, :] + jnp.concatenate([run] * nch, axis=1)
        s = lax.dot_general(q, _bf16(k_refs[p][...]), _NT, preferred_element_type=jnp.float32)
        scores[p] = jnp.where(own, s + bias, NEG)
        run = run + page_tot[p:p + 1, :]
    run_scr[...] = run
    smax = scores[0]
    for p in range(1, npp):
        smax = jnp.maximum(smax, scores[p])
    m_old = m_scr[...]
    m_new = jnp.maximum(m_old, jnp.max(smax, axis=-1, keepdims=True))
    alpha = jnp.exp(m_old - m_new)
    psum = jnp.zeros((rows, width), jnp.float32)
    acc = alpha * acc_scr[...]
    for p in range(npp):
        pr = jnp.exp(scores[p] - m_new)
        psum = psum + pr
        acc = acc + _dot(_bf16(pr), _bf16(v_refs[p][...]))
    l_new = alpha * l_scr[...] + jnp.sum(psum, axis=-1, keepdims=True)
    m_scr[...] = m_new
    l_scr[...] = l_new
    acc_scr[...] = acc

    @pl.when(g == pl.num_programs(1) - 1)
    def _():
        o_ref[...] = acc / l_new


def _decode_attention(page_table, q16, k_new16, v_new16, lf_row, kc, vc, lfc):
    nb, n_pages = page_table.shape
    npp = PAGES_PER_STEP
    assert n_pages % npp == 0
    ng = n_pages // npp
    width = kc.shape[1]
    rows = q16.shape[1]

    def page_map(p):
        return lambda b, g, pt: (pt[b, (ng - 1 - g) * npp + p], 0, 0)

    per_b = lambda b, g, pt: (b, 0, 0)
    in_specs = [pl.BlockSpec((None, rows, HEAD_DIM), per_b)] * 3
    in_specs.append(pl.BlockSpec((None, 1, LANE), per_b))
    in_specs += [pl.BlockSpec((None, width, HEAD_DIM), page_map(p)) for p in range(npp)]
    in_specs += [pl.BlockSpec((None, width, HEAD_DIM), page_map(p)) for p in range(npp)]
    in_specs += [pl.BlockSpec((None, 1, width), page_map(p)) for p in range(npp)]
    kern = functools.partial(_decode_kernel, npp=npp)
    grid_spec = pltpu.PrefetchScalarGridSpec(
        num_scalar_prefetch=1,
        grid=(nb, ng),
        in_specs=in_specs,
        out_specs=pl.BlockSpec((None, rows, HEAD_DIM), per_b),
        scratch_shapes=[pltpu.VMEM((rows, 1), jnp.float32),
                        pltpu.VMEM((rows, 1), jnp.float32),
                        pltpu.VMEM((rows, HEAD_DIM), jnp.float32),
                        pltpu.VMEM((1, LANE), jnp.float32),
                        pltpu.VMEM((npp, width), jnp.float32)],
    )
    return pl.pallas_call(
        kern,
        grid_spec=grid_spec,
        out_shape=jax.ShapeDtypeStruct((nb, rows, HEAD_DIM), jnp.float32),
        compiler_params=_params(("arbitrary", "arbitrary")),
        name="decode_attention",
    )(page_table, q16, k_new16, v_new16, lf_row, *([kc] * npp), *([vc] * npp), *([lfc] * npp))


def _outproj_kernel(c_ref, a_ref, x_ref, w1_ref, w2_ref, g_ref, h_ref, hn_ref):
    h = x_ref[...] + _dot(c_ref[...], w1_ref[...]) + _dot(a_ref[...], w2_ref[...])
    h_ref[...] = h
    hn_ref[...] = _bf16(_rms(h, g_ref[...]))


def _outproj(conv_out, attn_out, x, w_out, g, *, tm):
    m, d = x.shape
    half = conv_out.shape[1]
    return pl.pallas_call(
        _outproj_kernel,
        grid=(m // tm,),
        in_specs=[
            pl.BlockSpec((tm, half), lambda i: (i, 0)),
            pl.BlockSpec((tm, half), lambda i: (i, 0)),
            pl.BlockSpec((tm, d), lambda i: (i, 0)),
            pl.BlockSpec((half, d), lambda i: (0, 0)),
            pl.BlockSpec((half, d), lambda i: (1, 0)),
            pl.BlockSpec((1, d), lambda i: (0, 0)),
        ],
        out_specs=[
            pl.BlockSpec((tm, d), lambda i: (i, 0)),
            pl.BlockSpec((tm, d), lambda i: (i, 0)),
        ],
        out_shape=[
            jax.ShapeDtypeStruct((m, d), jnp.float32),
            jax.ShapeDtypeStruct((m, d), jnp.bfloat16),
        ],
        compiler_params=_params(("parallel",)),
        name="outproj",
    )(conv_out, attn_out, x, w_out, w_out, g)


def _ffn_prompt_kernel(hn_ref, h_ref, wa_ref, wb_ref, wd_ref, cwa_ref, cwb_ref,
                       cba_ref, cbb_ref, gf_ref,
                       y_ref, ta_ref, tb_ref, acc_scr, up_scr, tail_scr, *, tm, tf):
    i = pl.program_id(0)
    j = pl.program_id(1)
    nj = pl.num_programs(1)
    hist = SUBLANE
    slot = pl.multiple_of(j * hist, hist)

    @pl.when(j == 0)
    def _():
        acc_scr[...] = jnp.zeros_like(acc_scr)

    @pl.when(i == 0)
    def _():
        up_scr[0:hist, :] = jnp.zeros((hist, 2 * tf), jnp.float32)

    @pl.when(i > 0)
    def _():
        up_scr[0:hist, :] = tail_scr[pl.ds(slot, hist), :]

    hn = hn_ref[...]
    up_scr[hist:hist + tm, 0:tf] = _dot(hn, wa_ref[...])
    up_scr[hist:hist + tm, tf:2 * tf] = _dot(hn, wb_ref[...])

    def conv(cols, cw_ref, cb_ref):
        return (cb_ref[...]
                + cw_ref[0:1, :] * up_scr[hist - 2:hist - 2 + tm, cols]
                + cw_ref[1:2, :] * up_scr[hist - 1:hist - 1 + tm, cols]
                + cw_ref[2:3, :] * up_scr[hist:hist + tm, cols])

    ca = conv(slice(0, tf), cwa_ref, cba_ref)
    cb = conv(slice(tf, 2 * tf), cwb_ref, cbb_ref)
    act = _bf16(ca * _sigmoid(ca) * cb)
    acc_scr[...] += _dot(act, wd_ref[...])

    tail = up_scr[tm:tm + hist, :]
    tail_scr[pl.ds(slot, hist), :] = tail

    @pl.when(i == pl.num_programs(0) - 1)
    def _():
        cols = pl.ds(pl.multiple_of(j * tf, tf), tf)
        ta_ref[:, cols] = tail[:, 0:tf]
        tb_ref[:, cols] = tail[:, tf:2 * tf]

    @pl.when(j == nj - 1)
    def _():
        y_ref[...] = _rms(h_ref[...] + acc_scr[...], gf_ref[...])


def _ffn_prompt(hn, h, w_up, w_down, cw, cb, gf, *, tm, tf):
    m, d = h.shape
    dff = w_down.shape[0]
    assert dff % tf == 0 and m % tm == 0
    nj = dff // tf
    kern = functools.partial(_ffn_prompt_kernel, tm=tm, tf=tf)
    return pl.pallas_call(
        kern,
        grid=(m // tm, nj),
        in_specs=[
            pl.BlockSpec((tm, d), lambda i, j: (i, 0)),
            pl.BlockSpec((tm, d), lambda i, j: (i, 0)),
            pl.BlockSpec((None, d, tf), lambda i, j: (j, 0, 0)),
            pl.BlockSpec((None, d, tf), lambda i, j: (j + nj, 0, 0)),
            pl.BlockSpec((tf, d), lambda i, j: (j, 0)),
            pl.BlockSpec((3, tf), lambda i, j: (0, j)),
            pl.BlockSpec((3, tf), lambda i, j: (0, j + nj)),
            pl.BlockSpec((1, tf), lambda i, j: (0, j)),
            pl.BlockSpec((1, tf), lambda i, j: (0, j + nj)),
            pl.BlockSpec((1, d), lambda i, j: (0, 0)),
        ],
        out_specs=[
            pl.BlockSpec((tm, d), lambda i, j: (i, 0)),
            pl.BlockSpec((SUBLANE, dff), lambda i, j: (0, 0)),
            pl.BlockSpec((SUBLANE, dff), lambda i, j: (0, 0)),
        ],
        out_shape=[
            jax.ShapeDtypeStruct((m, d), jnp.float32),
            jax.ShapeDtypeStruct((SUBLANE, dff), jnp.float32),
            jax.ShapeDtypeStruct((SUBLANE, dff), jnp.float32),
        ],
        scratch_shapes=[pltpu.VMEM((tm, d), jnp.float32),
                        pltpu.VMEM((SUBLANE + tm, 2 * tf), jnp.float32),
                        pltpu.VMEM((nj * SUBLANE, 2 * tf), jnp.float32)],
        compiler_params=_params(("arbitrary", "arbitrary")),
        name="ffn_prompt",
    )(hn, h, w_up, w_up, w_down, cw, cw, cb, cb, gf)


def _ffn_sample_kernel(hn_ref, h_ref, wa_ref, wb_ref, wd_ref, sta_ref, stb_ref,
                       cwa_ref, cwb_ref, cba_ref, cbb_ref, gf_ref,
                       y_ref, ua_ref, ub_ref, acc_scr):
    j = pl.program_id(0)

    @pl.when(j == 0)
    def _():
        acc_scr[...] = jnp.zeros_like(acc_scr)

    hn = hn_ref[...]
    ua = _dot(hn, wa_ref[...])
    ub = _dot(hn, wb_ref[...])
    ua_ref[...] = ua
    ub_ref[...] = ub
    ca = cba_ref[...] + cwa_ref[0:1, :] * sta_ref[0] + cwa_ref[1:2, :] * sta_ref[1] + cwa_ref[2:3, :] * ua
    cb = cbb_ref[...] + cwb_ref[0:1, :] * stb_ref[0] + cwb_ref[1:2, :] * stb_ref[1] + cwb_ref[2:3, :] * ub
    act = _bf16(ca * _sigmoid(ca) * cb)
    acc_scr[...] += _dot(act, wd_ref[...])

    @pl.when(j == pl.num_programs(0) - 1)
    def _():
        y_ref[...] = _rms(h_ref[...] + acc_scr[...], gf_ref[...])


def _ffn_sample(hn, h, w_up, w_down, state_t, cw, cb, gf, *, tf):
    m, d = h.shape
    dff = w_down.shape[0]
    nj = dff // tf
    return pl.pallas_call(
        _ffn_sample_kernel,
        grid=(nj,),
        in_specs=[
            pl.BlockSpec((m, d), lambda j: (0, 0)),
            pl.BlockSpec((m, d), lambda j: (0, 0)),
            pl.BlockSpec((None, d, tf), lambda j: (j, 0, 0)),
            pl.BlockSpec((None, d, tf), lambda j: (j + nj, 0, 0)),
            pl.BlockSpec((tf, d), lambda j: (j, 0)),
            pl.BlockSpec((2, m, tf), lambda j: (0, 0, j)),
            pl.BlockSpec((2, m, tf), lambda j: (0, 0, j + nj)),
            pl.BlockSpec((3, tf), lambda j: (0, j)),
            pl.BlockSpec((3, tf), lambda j: (0, j + nj)),
            pl.BlockSpec((1, tf), lambda j: (0, j)),
            pl.BlockSpec((1, tf), lambda j: (0, j + nj)),
            pl.BlockSpec((1, d), lambda j: (0, 0)),
        ],
        out_specs=[
            pl.BlockSpec((m, d), lambda j: (0, 0)),
            pl.BlockSpec((m, tf), lambda j: (0, j)),
            pl.BlockSpec((m, tf), lambda j: (0, j)),
        ],
        out_shape=[
            jax.ShapeDtypeStruct((m, d), jnp.float32),
            jax.ShapeDtypeStruct((m, dff), jnp.float32),
            jax.ShapeDtypeStruct((m, dff), jnp.float32),
        ],
        scratch_shapes=[pltpu.VMEM((m, d), jnp.float32)],
        compiler_params=_params(("arbitrary",)),
        name="ffn_sample",
    )(hn, h, w_up, w_up, w_down, state_t, state_t, cw, cw, cb, cb, gf)


def _pick(n, prefs):
    for p in prefs:
        if n % p == 0:
            return p
    return n


def kernel(x_prompt, x_sample, cache_k, cache_v, cache_logf, state_conv, state_ffn, page_table,
           norm_mix_g, w_in, b_f, conv_dw_w, conv_dw_b, conv_ln_g, conv_ln_b, w_out,
           norm_ffn_g, w_up, ffn_dw_w, ffn_dw_b, w_down, norm_final_g):
    f32 = jnp.float32
    depth = w_in.shape[0]
    assert depth == 1 and x_prompt.shape[0] == 1 and x_sample.shape[1] == 1
    s, d = x_prompt.shape[1], x_prompt.shape[2]
    nb = x_sample.shape[0]
    n_pool, page = cache_k.shape[1], cache_k.shape[2]
    assert cache_k.shape[3:] == (N_HEADS, HEAD_DIM)
    dff = w_down.shape[1]
    conv_ch = conv_dw_w.shape[2]
    assert conv_ch == ATTN_WIDTH
    n_main = 2 * conv_ch + 3 * ATTN_WIDTH

    w_in_bf = _bf16(w_in[0])
    w_main = w_in_bf[:, :n_main]
    q0, v0 = 2 * conv_ch, 2 * conv_ch + 2 * ATTN_WIDTH
    wt_qv = jnp.transpose(jnp.concatenate(
        [w_main[:, q0:q0 + ATTN_WIDTH], w_main[:, v0:v0 + ATTN_WIDTH]], axis=1))
    wf_pad = jnp.pad(w_in_bf[:, n_main:], ((0, 0), (0, LANE - N_HEADS)))
    bf_row = jnp.pad(b_f[0].reshape(1, N_HEADS), ((0, 0), (0, LANE - N_HEADS)))
    w_out_bf = _bf16(w_out[0])
    w_down_bf = _bf16(w_down[0])
    tf = _pick(dff, (512, 256))
    tn_in = 512

    def col_tiles(w, t):
        return jnp.transpose(w.reshape(w.shape[0], w.shape[1] // t, t), (1, 0, 2))

    w_tiles = col_tiles(w_main, tn_in)
    w_up_tiles = col_tiles(_bf16(w_up[0]), tf)
    row = lambda v: v.reshape(1, -1)
    g_mix, g_ffn, g_fin = row(norm_mix_g[0]), row(norm_ffn_g[0]), row(norm_final_g)
    cw, cb_, lng, lnb = conv_dw_w[0], row(conv_dw_b[0]), row(conv_ln_g[0]), row(conv_ln_b[0])
    fcw, fcb = ffn_dw_w[0], row(ffn_dw_b[0])
    kw = cw.shape[0]
    tril = _bf16(jnp.tril(jnp.ones((SCAN_CHUNK, SCAN_CHUNK), f32)))
    src = jnp.arange(3 * LANE)
    dst = jnp.arange(ATTN_WIDTH)
    psel = _bf16(((src[:, None] % LANE == dst[None, :] // LANE)
                  & (src[:, None] // LANE == dst[None, :] % LANE)
                  & (src[:, None] % LANE < N_HEADS)).astype(f32))

    xp = x_prompt[0]
    ag_p, kv_p, qat_p, ka_p, vat_p, lf_p = _inproj_prompt(
        xp, g_mix, w_tiles, wt_qv, wf_pad, bf_row, tril, psel, tm=_pick(s, (512, 256)))
    conv_p, utail_p = _conv_prompt(ag_p, cw, cb_, lng, lnb, tb=_pick(s, (256, 128)))
    attn_p = _flash_prompt(qat_p, ka_p, vat_p)
    h_p, hn_p = _outproj(conv_p, attn_p, xp, w_out_bf, g_ffn, tm=_pick(s, (512, 256)))
    y_p, ta_p, tb_p = _ffn_prompt(hn_p, h_p, w_up_tiles, w_down_bf, fcw, fcb, g_fin,
                                  tm=_pick(s, (512, 256)), tf=tf)

    y_prompt = y_p[None]
    k_prompt = kv_p[:, :ATTN_WIDTH].reshape(1, 1, s, N_HEADS, HEAD_DIM)
    v_prompt = kv_p[:, ATTN_WIDTH:].reshape(1, 1, s, N_HEADS, HEAD_DIM)
    logf_prompt = lf_p[:, :N_HEADS].reshape(1, 1, s, N_HEADS)
    conv_prompt = utail_p[utail_p.shape[0] - (kw - 1):][None, None]
    ffn_prompt = jnp.concatenate([ta_p[SUBLANE - 2:], tb_p[SUBLANE - 2:]], axis=1)[None, None]

    xs = x_sample[:, 0, :]
    ag_s, kv_s, q_s, lf_s128 = _inproj_sample(xs, g_mix, w_main, wf_pad, bf_row)
    conv_s, u_s = _conv_sample(ag_s, jnp.swapaxes(state_conv[0], 0, 1), cw, cb_, lng, lnb)

    pad_heads = lambda a: jnp.pad(a.reshape(nb, N_HEADS, HEAD_DIM),
                                  ((0, 0), (0, HEAD_ROWS - N_HEADS), (0, 0)))
    q16 = pad_heads(q_s)
    kn16 = pad_heads(kv_s[:, :ATTN_WIDTH])
    vn16 = pad_heads(kv_s[:, ATTN_WIDTH:])
    lf_s = lf_s128[:, :N_HEADS]
    lf_row = jnp.tile(lf_s, (1, LANE // N_HEADS)).reshape(nb, 1, LANE)
    kc = cache_k[0].reshape(n_pool, page * N_HEADS, HEAD_DIM)
    vc = cache_v[0].reshape(n_pool, page * N_HEADS, HEAD_DIM)
    lfc = cache_logf[0].astype(f32).reshape(n_pool, 1, page * N_HEADS)
    attn_s16 = _decode_attention(page_table, q16, kn16, vn16, lf_row, kc, vc, lfc)
    attn_s = _bf16(attn_s16[:, :N_HEADS, :].reshape(nb, ATTN_WIDTH))

    h_s, hn_s = _outproj(conv_s, attn_s, xs, w_out_bf, g_ffn, tm=nb)
    y_s, ua_s, ub_s = _ffn_sample(hn_s, h_s, w_up_tiles, w_down_bf,
                                  jnp.swapaxes(state_ffn[0], 0, 1), fcw, fcb, g_fin, tf=tf)

    y_sample = y_s[:, None, :]
    k_sample = kv_s[:, :ATTN_WIDTH].reshape(1, nb, 1, N_HEADS, HEAD_DIM)
    v_sample = kv_s[:, ATTN_WIDTH:].reshape(1, nb, 1, N_HEADS, HEAD_DIM)
    logf_sample = lf_s.reshape(1, nb, 1, N_HEADS)
    conv_sample = jnp.concatenate([state_conv[0][:, 1:, :], u_s[:, None, :]], axis=1)[None]
    up_s = jnp.concatenate([ua_s, ub_s], axis=1)
    ffn_sample = jnp.concatenate([state_ffn[0][:, 1:, :], up_s[:, None, :]], axis=1)[None]

    return (y_prompt, y_sample, k_prompt, v_prompt, logf_prompt, conv_prompt, ffn_prompt,
            k_sample, v_sample, logf_sample, conv_sample, ffn_sample)
```

```python
import functools
import math

import jax
import jax.numpy as jnp
from jax import lax
from jax.experimental import pallas as pl
from jax.experimental.pallas import tpu as pltpu

EPS = 1e-6
LANE = 128
SUBLANE = 8
MXU_DIM = 256
HEAD_DIM = 128
N_HEADS = 8
ATTN_WIDTH = N_HEADS * HEAD_DIM
HEAD_ROWS = 16
QK_DEPTH = MXU_DIM
V_ROWS = HEAD_DIM + 16
SCAN_CHUNK = 256
PAGES_PER_STEP = 8
VMEM_LIMIT = 56 * 1024 * 1024
LOG2E = math.log2(math.e)
NEG = -1e30

_NT = (((1,), (1,)), ((), ()))


def _bf16(x):
    return x.astype(jnp.bfloat16)


def _dot(a, b):
    return jnp.dot(a, b, preferred_element_type=jnp.float32)


def _split3(x):
    hi = _bf16(x)
    r1 = x - hi.astype(jnp.float32)
    mid = _bf16(r1)
    lo = _bf16(r1 - mid.astype(jnp.float32))
    return hi, mid, lo


def _log_sigmoid(x):
    return jnp.minimum(x, 0.0) - jnp.log1p(jnp.exp(-jnp.abs(x)))


def _sigmoid(x):
    return 1.0 / (1.0 + jnp.exp(-x))


def _rms(x, g):
    ms = jnp.mean(x * x, axis=-1, keepdims=True)
    return x * lax.rsqrt(ms + EPS) * g


def _params(sem):
    return pltpu.CompilerParams(dimension_semantics=sem, vmem_limit_bytes=VMEM_LIMIT)


def _inproj_prompt_kernel(x_ref, g_ref, w_ref, wt_ref, wf_ref, bf_ref, tril_ref, psel_ref,
                          ag_ref, kv_ref, qat_ref, ka_ref, vat_ref, lf_ref,
                          xn_scr, aug_scr, carry_scr, *, tm, tn, q_scale):
    i = pl.program_id(0)
    j = pl.program_id(1)
    hpt = tn // HEAD_DIM
    tps = ATTN_WIDTH // tn

    @pl.when(j == 0)
    def _():
        xn_scr[...] = _bf16(_rms(x_ref[...], g_ref[...]))
        lf = _log_sigmoid(_dot(xn_scr[...], wf_ref[...]) + bf_ref[...])
        lf_ref[...] = lf

        @pl.when(i == 0)
        def _():
            carry_scr[...] = jnp.zeros_like(carry_scr)

        carry = carry_scr[0:1, :]
        tril = tril_ref[...]
        for c in range(tm // SCAN_CHUNK):
            rows = slice(c * SCAN_CHUNK, (c + 1) * SCAN_CHUNK)
            hi, mid, lo = _split3(lf[rows, :])
            cs = _dot(tril, hi) + _dot(tril, mid) + _dot(tril, lo) + carry
            h2, m2, l2 = _split3(cs * (-LOG2E))
            aug = (_dot(h2, psel_ref[0:LANE, :]) + _dot(m2, psel_ref[LANE:2 * LANE, :])
                   + _dot(l2, psel_ref[2 * LANE:3 * LANE, :]))
            aug_scr[rows, :] = _bf16(aug)
            carry = cs[SCAN_CHUNK - 1:SCAN_CHUNK, :]
        carry_scr[...] = jnp.broadcast_to(carry, carry_scr.shape)

    def nn():
        return _dot(xn_scr[...], w_ref[...])

    def nt():
        return lax.dot_general(wt_ref[...], xn_scr[...], _NT,
                               preferred_element_type=jnp.float32)

    @pl.when(j < 2 * tps)
    def _():
        ag_ref[...] = nn()

    for t in range(tps):
        @pl.when(j == 2 * tps + t)
        def _(t=t):
            zt = nt() * q_scale
            ones = (lax.broadcasted_iota(jnp.int32, (QK_DEPTH - HEAD_DIM, tm), 0) < 3
                    ).astype(jnp.bfloat16)
            for hh in range(hpt):
                r = (t * hpt + hh) * QK_DEPTH
                qat_ref[r:r + HEAD_DIM, :] = _bf16(zt[hh * HEAD_DIM:(hh + 1) * HEAD_DIM, :])
                qat_ref[r + HEAD_DIM:r + QK_DEPTH, :] = ones

        @pl.when(j == 3 * tps + t)
        def _(t=t):
            z = nn()
            for hh in range(hpt):
                h = t * hpt + hh
                c = h * QK_DEPTH
                zh = z[:, hh * HEAD_DIM:(hh + 1) * HEAD_DIM]
                kv_ref[:, h, :] = zh
                ka_ref[:, c:c + HEAD_DIM] = _bf16(zh)
                ka_ref[:, c + HEAD_DIM:c + QK_DEPTH] = aug_scr[:, h * LANE:(h + 1) * LANE]

        @pl.when(j == 4 * tps + t)
        def _(t=t):
            z = nn()
            for hh in range(hpt):
                kv_ref[:, t * hpt + hh, :] = z[:, hh * HEAD_DIM:(hh + 1) * HEAD_DIM]
            zt = nt()
            ones = (lax.broadcasted_iota(jnp.int32, (V_ROWS - HEAD_DIM, tm), 0) == 0
                    ).astype(jnp.bfloat16)
            for hh in range(hpt):
                r = (t * hpt + hh) * V_ROWS
                vat_ref[r:r + HEAD_DIM, :] = _bf16(zt[hh * HEAD_DIM:(hh + 1) * HEAD_DIM, :])
                vat_ref[r + HEAD_DIM:r + V_ROWS, :] = ones


def _inproj_prompt(x, g, w_main, wt_qv, wf_pad, bf_row, tril, psel, *, tm, tn):
    m, d = x.shape
    tps = ATTN_WIDTH // tn
    nj = 5 * tps
    assert w_main.shape == (d, 5 * ATTN_WIDTH) and m % tm == 0 and tm % SCAN_CHUNK == 0
    kern = functools.partial(_inproj_prompt_kernel, tm=tm, tn=tn,
                             q_scale=HEAD_DIM ** -0.5 * LOG2E)

    def w_map(i, j):
        return (0, jnp.where((j >= 2 * tps) & (j < 3 * tps), 2 * tps - 1, j))

    def wt_map(i, j):
        t = jnp.clip(j - 2 * tps, 0, tps - 1) + jnp.where(j >= 3 * tps, 1, 0) \
            + jnp.clip(j - 4 * tps, 0, tps - 1)
        return (jnp.minimum(t, 2 * tps - 1), 0)

    return pl.pallas_call(
        kern,
        grid=(m // tm, nj),
        in_specs=[
            pl.BlockSpec((tm, d), lambda i, j: (i, 0)),
            pl.BlockSpec((1, d), lambda i, j: (0, 0)),
            pl.BlockSpec((d, tn), w_map),
            pl.BlockSpec((tn, d), wt_map),
            pl.BlockSpec((d, LANE), lambda i, j: (0, 0)),
            pl.BlockSpec((1, LANE), lambda i, j: (0, 0)),
            pl.BlockSpec((SCAN_CHUNK, SCAN_CHUNK), lambda i, j: (0, 0)),
            pl.BlockSpec((3 * LANE, ATTN_WIDTH), lambda i, j: (0, 0)),
        ],
        out_specs=[
            pl.BlockSpec((tm, tn), lambda i, j: (i, jnp.minimum(j, 2 * tps - 1))),
            pl.BlockSpec((None, tm, N_HEADS, HEAD_DIM),
                         lambda i, j: (jnp.where(j >= 4 * tps, 1, 0), i, 0, 0)),
            pl.BlockSpec((None, N_HEADS * QK_DEPTH, tm), lambda i, j: (i, 0, 0)),
            pl.BlockSpec((tm, N_HEADS * QK_DEPTH), lambda i, j: (i, 0)),
            pl.BlockSpec((None, N_HEADS * V_ROWS, tm), lambda i, j: (i, 0, 0)),
            pl.BlockSpec((tm, LANE), lambda i, j: (i, 0)),
        ],
        out_shape=[
            jax.ShapeDtypeStruct((m, 2 * ATTN_WIDTH), jnp.float32),
            jax.ShapeDtypeStruct((2, m, N_HEADS, HEAD_DIM), jnp.float32),
            jax.ShapeDtypeStruct((m // tm, N_HEADS * QK_DEPTH, tm), jnp.bfloat16),
            jax.ShapeDtypeStruct((m, N_HEADS * QK_DEPTH), jnp.bfloat16),
            jax.ShapeDtypeStruct((m // tm, N_HEADS * V_ROWS, tm), jnp.bfloat16),
            jax.ShapeDtypeStruct((m, LANE), jnp.float32),
        ],
        scratch_shapes=[pltpu.VMEM((tm, d), jnp.bfloat16),
                        pltpu.VMEM((tm, ATTN_WIDTH), jnp.bfloat16),
                        pltpu.VMEM((SUBLANE, LANE), jnp.float32)],
        compiler_params=_params(("arbitrary", "arbitrary")),
        name="inproj_prompt",
    )(x, g, w_main, wt_qv, wf_pad, bf_row, tril, psel)


def _inproj_sample_kernel(x_ref, g_ref, w_ref, wf_ref, bf_ref,
                          ag_ref, kv_ref, q_ref, lf_ref, xn_scr, *, q_scale):
    j = pl.program_id(0)

    @pl.when(j == 0)
    def _():
        xn_scr[...] = _bf16(_rms(x_ref[...], g_ref[...]))
        lf_ref[...] = _log_sigmoid(_dot(xn_scr[...], wf_ref[...]) + bf_ref[...])

    z = _dot(xn_scr[...], w_ref[...])

    @pl.when(j < 2)
    def _():
        ag_ref[...] = z

    @pl.when(j == 2)
    def _():
        q_ref[...] = _bf16(z * q_scale)

    @pl.when(j > 2)
    def _():
        kv_ref[...] = z


def _inproj_sample(x, g, w_main, wf_pad, bf_row):
    m, d = x.shape
    tn = ATTN_WIDTH
    kern = functools.partial(_inproj_sample_kernel, q_scale=HEAD_DIM ** -0.5)
    return pl.pallas_call(
        kern,
        grid=(5,),
        in_specs=[
            pl.BlockSpec((m, d), lambda j: (0, 0)),
            pl.BlockSpec((1, d), lambda j: (0, 0)),
            pl.BlockSpec((d, tn), lambda j: (0, j)),
            pl.BlockSpec((d, LANE), lambda j: (0, 0)),
            pl.BlockSpec((1, LANE), lambda j: (0, 0)),
        ],
        out_specs=[
            pl.BlockSpec((m, tn), lambda j: (0, jnp.minimum(j, 1))),
            pl.BlockSpec((m, tn), lambda j: (0, jnp.clip(j - 3, 0, 1))),
            pl.BlockSpec((m, tn), lambda j: (0, 0)),
            pl.BlockSpec((m, LANE), lambda j: (0, 0)),
        ],
        out_shape=[
            jax.ShapeDtypeStruct((m, 2 * tn), jnp.float32),
            jax.ShapeDtypeStruct((m, 2 * tn), jnp.float32),
            jax.ShapeDtypeStruct((m, tn), jnp.bfloat16),
            jax.ShapeDtypeStruct((m, LANE), jnp.float32),
        ],
        scratch_shapes=[pltpu.VMEM((m, d), jnp.bfloat16)],
        compiler_params=_params(("arbitrary",)),
        name="inproj_sample",
    )(x, g, w_main, wf_pad, bf_row)


def _layernorm_swish(y, g, b):
    mu = jnp.mean(y, axis=-1, keepdims=True)
    yc = y - mu
    var = jnp.mean(yc * yc, axis=-1, keepdims=True)
    yn = yc * lax.rsqrt(var + EPS) * g + b
    return yn * _sigmoid(yn)


def _conv_prompt_kernel(a_ref, gate_ref, w_ref, b_ref, lng_ref, lnb_ref,
                        out_ref, tail_ref, uext_scr, ush_scr, y_scr, *, tb, kw, hist):
    i = pl.program_id(0)

    @pl.when(i == 0)
    def _():
        uext_scr[0:hist, :] = jnp.zeros((hist, uext_scr.shape[1]), jnp.float32)

    uext_scr[hist:hist + tb, :] = a_ref[...] * _sigmoid(gate_ref[...])
    span = ush_scr.shape[1]
    for sh in range(1, SUBLANE):
        ush_scr[sh - 1] = uext_scr[sh:sh + span, :]
    ch = a_ref.shape[1]
    rows = 128
    off = hist - (kw - 1)
    nrc = tb // rows
    def lane_chunk(cc, carry):
        cs = pl.ds(pl.multiple_of(cc * LANE, LANE), LANE)
        accs = [jnp.broadcast_to(b_ref[:, cs], (rows, LANE))] * nrc
        for t in range(kw):
            sh = (off + t) % SUBLANE
            wt = w_ref[t:t + 1, cs]
            for rc in range(nrc):
                r0 = rc * rows + (off + t) - sh
                if sh == 0:
                    u = uext_scr[r0:r0 + rows, cs]
                else:
                    u = ush_scr[sh - 1, r0:r0 + rows, cs]
                accs[rc] = accs[rc] + wt * u
        for rc in range(nrc):
            y_scr[rc * rows:(rc + 1) * rows, cs] = accs[rc]
        return carry

    lax.fori_loop(0, ch // LANE, lane_chunk, 0)
    out_ref[...] = _bf16(_layernorm_swish(y_scr[...], lng_ref[...], lnb_ref[...]))
    tail = uext_scr[tb:tb + hist, :]
    uext_scr[0:hist, :] = tail
    tail_ref[...] = tail


def _conv_prompt(ag, w, b, lng, lnb, *, tb):
    s = ag.shape[0]
    ch = ag.shape[1] // 2
    kw = w.shape[0]
    hist = 32
    assert kw - 1 <= hist and s % tb == 0
    kern = functools.partial(_conv_prompt_kernel, tb=tb, kw=kw, hist=hist)
    return pl.pallas_call(
        kern,
        grid=(s // tb,),
        in_specs=[
            pl.BlockSpec((tb, ch), lambda i: (i, 0)),
            pl.BlockSpec((tb, ch), lambda i: (i, 1)),
            pl.BlockSpec((kw, ch), lambda i: (0, 0)),
            pl.BlockSpec((1, ch), lambda i: (0, 0)),
            pl.BlockSpec((1, ch), lambda i: (0, 0)),
            pl.BlockSpec((1, ch), lambda i: (0, 0)),
        ],
        out_specs=[
            pl.BlockSpec((tb, ch), lambda i: (i, 0)),
            pl.BlockSpec((hist, ch), lambda i: (0, 0)),
        ],
        out_shape=[
            jax.ShapeDtypeStruct((s, ch), jnp.bfloat16),
            jax.ShapeDtypeStruct((hist, ch), jnp.float32),
        ],
        scratch_shapes=[pltpu.VMEM((hist + tb, ch), jnp.float32),
                        pltpu.VMEM((SUBLANE - 1, hist + tb - SUBLANE, ch), jnp.float32),
                        pltpu.VMEM((tb, ch), jnp.float32)],
        compiler_params=_params(("arbitrary",)),
        name="conv_prompt",
    )(ag, ag, w, b, lng, lnb)


def _conv_sample_kernel(a_ref, gate_ref, st_ref, w_ref, b_ref, lng_ref, lnb_ref,
                        out_ref, u_ref, *, kw):
    u = a_ref[...] * _sigmoid(gate_ref[...])
    u_ref[...] = u
    acc = b_ref[...] + w_ref[kw - 1:kw, :] * u
    for t in range(kw - 1):
        acc = acc + w_ref[t:t + 1, :] * st_ref[t]
    out_ref[...] = _bf16(_layernorm_swish(acc, lng_ref[...], lnb_ref[...]))


def _conv_sample(ag, state_t, w, b, lng, lnb):
    nb = ag.shape[0]
    ch = ag.shape[1] // 2
    kw = w.shape[0]
    kern = functools.partial(_conv_sample_kernel, kw=kw)
    return pl.pallas_call(
        kern,
        grid=(1,),
        in_specs=[
            pl.BlockSpec((nb, ch), lambda i: (0, 0)),
            pl.BlockSpec((nb, ch), lambda i: (0, 1)),
            pl.BlockSpec((kw - 1, nb, ch), lambda i: (0, 0, 0)),
            pl.BlockSpec((kw, ch), lambda i: (0, 0)),
            pl.BlockSpec((1, ch), lambda i: (0, 0)),
            pl.BlockSpec((1, ch), lambda i: (0, 0)),
            pl.BlockSpec((1, ch), lambda i: (0, 0)),
        ],
        out_specs=[
            pl.BlockSpec((nb, ch), lambda i: (0, 0)),
            pl.BlockSpec((nb, ch), lambda i: (0, 0)),
        ],
        out_shape=[
            jax.ShapeDtypeStruct((nb, ch), jnp.bfloat16),
            jax.ShapeDtypeStruct((nb, ch), jnp.float32),
        ],
        compiler_params=_params(("arbitrary",)),
        name="conv_sample",
    )(ag, ag, state_t, w, b, lng, lnb)


def _flash_kernel(qat_ref, ka_ref, vat_ref, o_ref, s_scr, acc_scr, *, tk):
    i = pl.program_id(1)
    tq = 2 * tk
    qat = jnp.concatenate([qat_ref[0], qat_ref[1]], axis=1)
    acc_scr[...] = jnp.zeros_like(acc_scr)

    def block_of(k):
        return jnp.where(k < 2, 2 * i + k, k - 2)

    def scores(k, slot, diag=None):
        j = k - 2 if diag is None else 2 * i + diag
        r0 = pl.multiple_of(j * tk, tk)
        st = _dot(ka_ref[pl.ds(r0, tk), :], qat)
        if diag is not None:
            key = lax.broadcasted_iota(jnp.int32, (tk, tq), 0) + diag * tk
            qry = lax.broadcasted_iota(jnp.int32, (tk, tq), 1)
            st = jnp.where(key <= qry, st, NEG)
        s_scr[slot] = st
        return jnp.max(st, axis=0, keepdims=True)

    def accumulate(k, slot, m_old, mx):
        m_new = jnp.maximum(m_old, mx)
        alpha = jnp.exp2(m_old - m_new)
        pt = _bf16(jnp.exp2(s_scr[slot] - m_new))
        acc_scr[...] = alpha * acc_scr[...] + _dot(vat_ref[block_of(k)], pt)
        return m_new

    def body(p, carry):
        m, mx0 = carry
        k = 2 * p
        mx1 = scores(k + 1, 1)
        m = accumulate(k, 0, m, mx0)
        mx0 = scores(k + 2, 0)
        m = accumulate(k + 1, 1, m, mx1)
        return m, mx0

    m0 = jnp.full((1, tq), NEG, jnp.float32)
    last = 2 * i + 1

    @pl.when(i == 0)
    def _():
        mx0 = scores(0, 0, diag=0)
        mx1 = scores(1, 1, diag=1)
        m = accumulate(0, 0, m0, mx0)
        accumulate(1, 1, m, mx1)

    @pl.when(i > 0)
    def _():
        mx0 = scores(0, 0, diag=0)
        mx1 = scores(1, 1, diag=1)
        m = accumulate(0, 0, m0, mx0)
        mx0 = scores(2, 0)
        m = accumulate(1, 1, m, mx1)
        m, mx0 = lax.fori_loop(1, i, body, (m, mx0))
        mx1 = scores(last, 1)
        m = accumulate(last - 1, 0, m, mx0)
        accumulate(last, 1, m, mx1)

    acc = acc_scr[...]
    out_t = acc[0:HEAD_DIM, :] / acc[HEAD_DIM:HEAD_DIM + 1, :]
    o_ref[...] = _bf16(jnp.transpose(out_t))


def _flash_prompt(qat, ka, vat):
    s = ka.shape[0]
    nblk, _, tk = qat.shape
    tq = 2 * tk
    assert nblk * tk == s and s % tq == 0 and vat.shape == (nblk, N_HEADS * V_ROWS, tk)
    kern = functools.partial(_flash_kernel, tk=tk)
    return pl.pallas_call(
        kern,
        grid=(N_HEADS, s // tq),
        in_specs=[
            pl.BlockSpec((2, QK_DEPTH, tk), lambda h, i: (i, h, 0)),
            pl.BlockSpec((s, QK_DEPTH), lambda h, i: (0, h)),
            pl.BlockSpec((nblk, V_ROWS, tk), lambda h, i: (0, h, 0)),
        ],
        out_specs=pl.BlockSpec((tq, HEAD_DIM), lambda h, i: (i, h)),
        out_shape=jax.ShapeDtypeStruct((s, ATTN_WIDTH), jnp.bfloat16),
        scratch_shapes=[pltpu.VMEM((2, tk, tq), jnp.float32),
                        pltpu.VMEM((V_ROWS, tq), jnp.float32)],
        compiler_params=_params(("parallel", "arbitrary")),
        name="flash_prompt",
    )(qat, ka, vat)


def _decode_kernel(pt_ref, q_ref, kn_ref, vn_ref, lfn_ref, *refs, npp):
    k_refs = refs[0:npp]
    v_refs = refs[npp:2 * npp]
    lf_refs = refs[2 * npp:3 * npp]
    o_ref = refs[3 * npp]
    m_scr, l_scr, acc_scr, run_scr, lf_scr = refs[3 * npp + 1:]
    g = pl.program_id(1)
    rows = q_ref.shape[0]
    width = k_refs[0].shape[0]
    nch = width // LANE
    q = q_ref[...]

    @pl.when(g == 0)
    def _():
        kn = _bf16(kn_ref[...]).astype(jnp.float32)
        s_new = jnp.sum(q.astype(jnp.float32) * kn, axis=-1, keepdims=True)
        m_scr[...] = s_new
        l_scr[...] = jnp.ones_like(l_scr)
        acc_scr[...] = _bf16(vn_ref[...]).astype(jnp.float32)
        run_scr[...] = lfn_ref[...]

    for p in range(npp):
        lf_scr[p:p + 1, :] = lf_refs[p][...]
    x = lf_scr[...]
    xs = jnp.concatenate([x[:, c * LANE:(c + 1) * LANE] for c in range(nch)], axis=0)
    lane = lax.broadcasted_iota(jnp.int32, xs.shape, 1)
    y = xs
    tot = xs
    for sh in (8, 16, 32, 64):
        y = y + jnp.where(lane + sh < LANE, pltpu.roll(y, LANE - sh, axis=1), 0.0)
        tot = tot + pltpu.roll(tot, sh, axis=1)
    excl = y - xs
    later = jnp.zeros((npp, LANE), jnp.float32)
    pieces = [None] * nch
    for c in reversed(range(nch)):
        pieces[c] = excl[c * npp:(c + 1) * npp, :] + later
        later = later + tot[c * npp:(c + 1) * npp, :]
    within = jnp.concatenate(pieces, axis=1)
    page_tot = later

    hrow = lax.broadcasted_iota(jnp.int32, (rows, width), 0)
    hcol = lax.broadcasted_iota(jnp.int32, (rows, width), 1) & (N_HEADS - 1)
    own = hrow == hcol
    run = run_scr[...]
    scores = [None] * npp
    for p in reversed(range(npp)):
        bias = within[p:p + 1, :] + jnp.concatenate([run] * nch, axis=1)
        s = lax.dot_general(q, _bf16(k_refs[p][...]), _NT, preferred_element_type=jnp.float32)
        scores[p] = jnp.where(own, s + bias, NEG)
        run = run + page_tot[p:p + 1, :]
    run_scr[...] = run
    smax = scores[0]
    for p in range(1, npp):
        smax = jnp.maximum(smax, scores[p])
    m_old = m_scr[...]
    m_new = jnp.maximum(m_old, jnp.max(smax, axis=-1, keepdims=True))
    alpha = jnp.exp(m_old - m_new)
    psum = jnp.zeros((rows, width), jnp.float32)
    acc = alpha * acc_scr[...]
    for p in range(npp):
        pr = jnp.exp(scores[p] - m_new)
        psum = psum + pr
        acc = acc + _dot(_bf16(pr), _bf16(v_refs[p][...]))
    l_new = alpha * l_scr[...] + jnp.sum(psum, axis=-1, keepdims=True)
    m_scr[...] = m_new
    l_scr[...] = l_new
    acc_scr[...] = acc

    @pl.when(g == pl.num_programs(1) - 1)
    def _():
        o_ref[...] = acc / l_new


def _decode_attention(page_table, q16, k_new16, v_new16, lf_row, kc, vc, lfc):
    nb, n_pages = page_table.shape
    npp = PAGES_PER_STEP
    assert n_pages % npp == 0
    ng = n_pages // npp
    width = kc.shape[1]
    rows = q16.shape[1]

    def page_map(p):
        return lambda b, g, pt: (pt[b, (ng - 1 - g) * npp + p], 0, 0)

    per_b = lambda b, g, pt: (b, 0, 0)
    in_specs = [pl.BlockSpec((None, rows, HEAD_DIM), per_b)] * 3
    in_specs.append(pl.BlockSpec((None, 1, LANE), per_b))
    in_specs += [pl.BlockSpec((None, width, HEAD_DIM), page_map(p % npp))
                 for p in range(2 * npp)]
    in_specs += [pl.BlockSpec((None, 1, width), page_map(p)) for p in range(npp)]
    kern = functools.partial(_decode_kernel, npp=npp)
    grid_spec = pltpu.PrefetchScalarGridSpec(
        num_scalar_prefetch=1,
        grid=(nb, ng),
        in_specs=in_specs,
        out_specs=pl.BlockSpec((None, rows, HEAD_DIM), per_b),
        scratch_shapes=[pltpu.VMEM((rows, 1), jnp.float32),
                        pltpu.VMEM((rows, 1), jnp.float32),
                        pltpu.VMEM((rows, HEAD_DIM), jnp.float32),
                        pltpu.VMEM((1, LANE), jnp.float32),
                        pltpu.VMEM((npp, width), jnp.float32)],
    )
    return pl.pallas_call(
        kern,
        grid_spec=grid_spec,
        out_shape=jax.ShapeDtypeStruct((nb, rows, HEAD_DIM), jnp.float32),
        compiler_params=_params(("arbitrary", "arbitrary")),
        name="decode_attention",
    )(page_table, q16, k_new16, v_new16, lf_row, *([kc] * npp), *([vc] * npp), *([lfc] * npp))


def _outproj_kernel(c_ref, a_ref, x_ref, w1_ref, w2_ref, g_ref, h_ref, hn_ref):
    h = x_ref[...] + _dot(c_ref[...], w1_ref[...]) + _dot(a_ref[...], w2_ref[...])
    h_ref[...] = h
    hn_ref[...] = _bf16(_rms(h, g_ref[...]))


def _outproj(conv_out, attn_out, x, w_out, g, *, tm):
    m, d = x.shape
    half = conv_out.shape[1]
    return pl.pallas_call(
        _outproj_kernel,
        grid=(m // tm,),
        in_specs=[
            pl.BlockSpec((tm, half), lambda i: (i, 0)),
            pl.BlockSpec((tm, half), lambda i: (i, 0)),
            pl.BlockSpec((tm, d), lambda i: (i, 0)),
            pl.BlockSpec((half, d), lambda i: (0, 0)),
            pl.BlockSpec((half, d), lambda i: (1, 0)),
            pl.BlockSpec((1, d), lambda i: (0, 0)),
        ],
        out_specs=[
            pl.BlockSpec((tm, d), lambda i: (i, 0)),
            pl.BlockSpec((tm, d), lambda i: (i, 0)),
        ],
        out_shape=[
            jax.ShapeDtypeStruct((m, d), jnp.float32),
            jax.ShapeDtypeStruct((m, d), jnp.bfloat16),
        ],
        compiler_params=_params(("parallel",)),
        name="outproj",
    )(conv_out, attn_out, x, w_out, w_out, g)


def _ffn_prompt_kernel(hn_ref, h_ref, wa_ref, wb_ref, wd_ref, cwa_ref, cwb_ref,
                       cba_ref, cbb_ref, gf_ref,
                       y_ref, ta_ref, tb_ref, acc_scr, up_scr, tail_scr, *, tm, tf):
    i = pl.program_id(0)
    j = pl.program_id(1)
    nj = pl.num_programs(1)
    hist = SUBLANE
    slot = pl.multiple_of(j * hist, hist)

    @pl.when(j == 0)
    def _():
        acc_scr[...] = jnp.zeros_like(acc_scr)

    @pl.when(i == 0)
    def _():
        up_scr[0:hist, :] = jnp.zeros((hist, 2 * tf), jnp.float32)

    @pl.when(i > 0)
    def _():
        up_scr[0:hist, :] = tail_scr[pl.ds(slot, hist), :]

    hn = hn_ref[...]
    up_scr[hist:hist + tm, 0:tf] = _dot(hn, wa_ref[...])
    up_scr[hist:hist + tm, tf:2 * tf] = _dot(hn, wb_ref[...])

    def conv(cols, cw_ref, cb_ref):
        return (cb_ref[...]
                + cw_ref[0:1, :] * up_scr[hist - 2:hist - 2 + tm, cols]
                + cw_ref[1:2, :] * up_scr[hist - 1:hist - 1 + tm, cols]
                + cw_ref[2:3, :] * up_scr[hist:hist + tm, cols])

    ca = conv(slice(0, tf), cwa_ref, cba_ref)
    cb = conv(slice(tf, 2 * tf), cwb_ref, cbb_ref)
    act = _bf16(ca * _sigmoid(ca) * cb)
    acc_scr[...] += _dot(act, wd_ref[...])

    tail = up_scr[tm:tm + hist, :]
    tail_scr[pl.ds(slot, hist), :] = tail

    @pl.when(i == pl.num_programs(0) - 1)
    def _():
        cols = pl.ds(pl.multiple_of(j * tf, tf), tf)
        ta_ref[:, cols] = tail[:, 0:tf]
        tb_ref[:, cols] = tail[:, tf:2 * tf]

    @pl.when(j == nj - 1)
    def _():
        y_ref[...] = _rms(h_ref[...] + acc_scr[...], gf_ref[...])


def _ffn_prompt(hn, h, w_up, w_down, cw, cb, gf, *, tm, tf):
    m, d = h.shape
    dff = w_down.shape[0]
    assert dff % tf == 0 and m % tm == 0
    nj = dff // tf
    kern = functools.partial(_ffn_prompt_kernel, tm=tm, tf=tf)
    return pl.pallas_call(
        kern,
        grid=(m // tm, nj),
        in_specs=[
            pl.BlockSpec((tm, d), lambda i, j: (i, 0)),
            pl.BlockSpec((tm, d), lambda i, j: (i, 0)),
            pl.BlockSpec((d, tf), lambda i, j: (0, j)),
            pl.BlockSpec((d, tf), lambda i, j: (0, j + nj)),
            pl.BlockSpec((tf, d), lambda i, j: (j, 0)),
            pl.BlockSpec((3, tf), lambda i, j: (0, j)),
            pl.BlockSpec((3, tf), lambda i, j: (0, j + nj)),
            pl.BlockSpec((1, tf), lambda i, j: (0, j)),
            pl.BlockSpec((1, tf), lambda i, j: (0, j + nj)),
            pl.BlockSpec((1, d), lambda i, j: (0, 0)),
        ],
        out_specs=[
            pl.BlockSpec((tm, d), lambda i, j: (i, 0)),
            pl.BlockSpec((SUBLANE, dff), lambda i, j: (0, 0)),
            pl.BlockSpec((SUBLANE, dff), lambda i, j: (0, 0)),
        ],
        out_shape=[
            jax.ShapeDtypeStruct((m, d), jnp.float32),
            jax.ShapeDtypeStruct((SUBLANE, dff), jnp.float32),
            jax.ShapeDtypeStruct((SUBLANE, dff), jnp.float32),
        ],
        scratch_shapes=[pltpu.VMEM((tm, d), jnp.float32),
                        pltpu.VMEM((SUBLANE + tm, 2 * tf), jnp.float32),
                        pltpu.VMEM((nj * SUBLANE, 2 * tf), jnp.float32)],
        compiler_params=_params(("arbitrary", "arbitrary")),
        name="ffn_prompt",
    )(hn, h, w_up, w_up, w_down, cw, cw, cb, cb, gf)


def _ffn_sample_kernel(hn_ref, h_ref, wa_ref, wb_ref, wd_ref, sta_ref, stb_ref,
                       cwa_ref, cwb_ref, cba_ref, cbb_ref, gf_ref,
                       y_ref, ua_ref, ub_ref, acc_scr):
    j = pl.program_id(0)

    @pl.when(j == 0)
    def _():
        acc_scr[...] = jnp.zeros_like(acc_scr)

    hn = hn_ref[...]
    ua = _dot(hn, wa_ref[...])
    ub = _dot(hn, wb_ref[...])
    ua_ref[...] = ua
    ub_ref[...] = ub
    ca = cba_ref[...] + cwa_ref[0:1, :] * sta_ref[0] + cwa_ref[1:2, :] * sta_ref[1] + cwa_ref[2:3, :] * ua
    cb = cbb_ref[...] + cwb_ref[0:1, :] * stb_ref[0] + cwb_ref[1:2, :] * stb_ref[1] + cwb_ref[2:3, :] * ub
    act = _bf16(ca * _sigmoid(ca) * cb)
    acc_scr[...] += _dot(act, wd_ref[...])

    @pl.when(j == pl.num_programs(0) - 1)
    def _():
        y_ref[...] = _rms(h_ref[...] + acc_scr[...], gf_ref[...])


def _ffn_sample(hn, h, w_up, w_down, state_t, cw, cb, gf, *, tf):
    m, d = h.shape
    dff = w_down.shape[0]
    nj = dff // tf
    return pl.pallas_call(
        _ffn_sample_kernel,
        grid=(nj,),
        in_specs=[
            pl.BlockSpec((m, d), lambda j: (0, 0)),
            pl.BlockSpec((m, d), lambda j: (0, 0)),
            pl.BlockSpec((d, tf), lambda j: (0, j)),
            pl.BlockSpec((d, tf), lambda j: (0, j + nj)),
            pl.BlockSpec((tf, d), lambda j: (j, 0)),
            pl.BlockSpec((2, m, tf), lambda j: (0, 0, j)),
            pl.BlockSpec((2, m, tf), lambda j: (0, 0, j + nj)),
            pl.BlockSpec((3, tf), lambda j: (0, j)),
            pl.BlockSpec((3, tf), lambda j: (0, j + nj)),
            pl.BlockSpec((1, tf), lambda j: (0, j)),
            pl.BlockSpec((1, tf), lambda j: (0, j + nj)),
            pl.BlockSpec((1, d), lambda j: (0, 0)),
        ],
        out_specs=[
            pl.BlockSpec((m, d), lambda j: (0, 0)),
            pl.BlockSpec((m, tf), lambda j: (0, j)),
            pl.BlockSpec((m, tf), lambda j: (0, j)),
        ],
        out_shape=[
            jax.ShapeDtypeStruct((m, d), jnp.float32),
            jax.ShapeDtypeStruct((m, dff), jnp.float32),
            jax.ShapeDtypeStruct((m, dff), jnp.float32),
        ],
        scratch_shapes=[pltpu.VMEM((m, d), jnp.float32)],
        compiler_params=_params(("arbitrary",)),
        name="ffn_sample",
    )(hn, h, w_up, w_up, w_down, state_t, state_t, cw, cw, cb, cb, gf)


def _pick(n, prefs):
    for p in prefs:
        if n % p == 0:
            return p
    return n


def kernel(x_prompt, x_sample, cache_k, cache_v, cache_logf, state_conv, state_ffn, page_table,
           norm_mix_g, w_in, b_f, conv_dw_w, conv_dw_b, conv_ln_g, conv_ln_b, w_out,
           norm_ffn_g, w_up, ffn_dw_w, ffn_dw_b, w_down, norm_final_g):
    f32 = jnp.float32
    depth = w_in.shape[0]
    assert depth == 1 and x_prompt.shape[0] == 1 and x_sample.shape[1] == 1
    s, d = x_prompt.shape[1], x_prompt.shape[2]
    nb = x_sample.shape[0]
    n_pool, page = cache_k.shape[1], cache_k.shape[2]
    assert cache_k.shape[3:] == (N_HEADS, HEAD_DIM)
    dff = w_down.shape[1]
    conv_ch = conv_dw_w.shape[2]
    assert conv_ch == ATTN_WIDTH
    n_main = 2 * conv_ch + 3 * ATTN_WIDTH

    w_in_bf = _bf16(w_in[0])
    w_main = w_in_bf[:, :n_main]
    q0, v0 = 2 * conv_ch, 2 * conv_ch + 2 * ATTN_WIDTH
    wt_qv = jnp.transpose(jnp.concatenate(
        [w_main[:, q0:q0 + ATTN_WIDTH], w_main[:, v0:v0 + ATTN_WIDTH]], axis=1))
    wf_pad = jnp.pad(w_in_bf[:, n_main:], ((0, 0), (0, LANE - N_HEADS)))
    bf_row = jnp.pad(b_f[0].reshape(1, N_HEADS), ((0, 0), (0, LANE - N_HEADS)))
    w_out_bf = _bf16(w_out[0])
    w_down_bf = _bf16(w_down[0])
    w_up_bf = _bf16(w_up[0])
    tf = _pick(dff, (512, 256))
    row = lambda v: v.reshape(1, -1)
    g_mix, g_ffn, g_fin = row(norm_mix_g[0]), row(norm_ffn_g[0]), row(norm_final_g)
    cw, cb_, lng, lnb = conv_dw_w[0], row(conv_dw_b[0]), row(conv_ln_g[0]), row(conv_ln_b[0])
    fcw, fcb = ffn_dw_w[0], row(ffn_dw_b[0])
    kw = cw.shape[0]
    tril = _bf16(jnp.tril(jnp.ones((SCAN_CHUNK, SCAN_CHUNK), f32)))
    src = jnp.arange(3 * LANE)
    dst = jnp.arange(ATTN_WIDTH)
    psel = _bf16(((src[:, None] % LANE == dst[None, :] // LANE)
                  & (src[:, None] // LANE == dst[None, :] % LANE)
                  & (src[:, None] % LANE < N_HEADS)).astype(f32))

    xp = x_prompt[0]
    ag_p, kv_p, qat_p, ka_p, vat_p, lf_p = _inproj_prompt(
        xp, g_mix, w_main, wt_qv, wf_pad, bf_row, tril, psel, tm=_pick(s, (512, 256)), tn=512)
    conv_p, utail_p = _conv_prompt(ag_p, cw, cb_, lng, lnb, tb=_pick(s, (256, 128)))
    attn_p = _flash_prompt(qat_p, ka_p, vat_p)
    h_p, hn_p = _outproj(conv_p, attn_p, xp, w_out_bf, g_ffn, tm=_pick(s, (512, 256)))
    y_p, ta_p, tb_p = _ffn_prompt(hn_p, h_p, w_up_bf, w_down_bf, fcw, fcb, g_fin,
                                  tm=_pick(s, (512, 256)), tf=tf)

    y_prompt = y_p[None]
    k_prompt = kv_p[0][None, None]
    v_prompt = kv_p[1][None, None]
    logf_prompt = lf_p[:, :N_HEADS].reshape(1, 1, s, N_HEADS)
    conv_prompt = utail_p[utail_p.shape[0] - (kw - 1):][None, None]
    ffn_prompt = jnp.concatenate([ta_p[SUBLANE - 2:], tb_p[SUBLANE - 2:]], axis=1)[None, None]

    xs = x_sample[:, 0, :]
    ag_s, kv_s, q_s, lf_s128 = _inproj_sample(xs, g_mix, w_main, wf_pad, bf_row)
    conv_s, u_s = _conv_sample(ag_s, jnp.swapaxes(state_conv[0], 0, 1), cw, cb_, lng, lnb)

    pad_heads = lambda a: jnp.pad(a.reshape(nb, N_HEADS, HEAD_DIM),
                                  ((0, 0), (0, HEAD_ROWS - N_HEADS), (0, 0)))
    q16 = pad_heads(q_s)
    kn16 = pad_heads(kv_s[:, :ATTN_WIDTH])
    vn16 = pad_heads(kv_s[:, ATTN_WIDTH:])
    lf_s = lf_s128[:, :N_HEADS]
    lf_row = jnp.tile(lf_s, (1, LANE // N_HEADS)).reshape(nb, 1, LANE)
    kc = cache_k[0].reshape(n_pool, page * N_HEADS, HEAD_DIM)
    vc = cache_v[0].reshape(n_pool, page * N_HEADS, HEAD_DIM)
    lfc = cache_logf[0].astype(f32).reshape(n_pool, 1, page * N_HEADS)
    attn_s16 = _decode_attention(page_table, q16, kn16, vn16, lf_row, kc, vc, lfc)
    attn_s = _bf16(attn_s16[:, :N_HEADS, :].reshape(nb, ATTN_WIDTH))

    h_s, hn_s = _outproj(conv_s, attn_s, xs, w_out_bf, g_ffn, tm=nb)
    y_s, ua_s, ub_s = _ffn_sample(hn_s, h_s, w_up_bf, w_down_bf,
                                  jnp.swapaxes(state_ffn[0], 0, 1), fcw, fcb, g_fin, tf=tf)

    y_sample = y_s[:, None, :]
    k_sample = kv_s[:, :ATTN_WIDTH].reshape(1, nb, 1, N_HEADS, HEAD_DIM)
    v_sample = kv_s[:, ATTN_WIDTH:].reshape(1, nb, 1, N_HEADS, HEAD_DIM)
    logf_sample = lf_s.reshape(1, nb, 1, N_HEADS)
    conv_sample = jnp.concatenate([state_conv[0][:, 1:, :], u_s[:, None, :]], axis=1)[None]
    up_s = jnp.concatenate([ua_s, ub_s], axis=1)
    ffn_sample = jnp.concatenate([state_ffn[0][:, 1:, :], up_s[:, None, :]], axis=1)[None]

    return (y_prompt, y_sample, k_prompt, v_prompt, logf_prompt, conv_prompt, ffn_prompt,
            k_sample, v_sample, logf_sample, conv_sample, ffn_sample)
```

```python
import functools
import math

import jax
import jax.numpy as jnp
from jax import lax
from jax.experimental import pallas as pl
from jax.experimental.pallas import tpu as pltpu

EPS = 1e-6
LANE = 128
SUBLANE = 8
MXU_DIM = 256
HEAD_DIM = 128
N_HEADS = 8
ATTN_WIDTH = N_HEADS * HEAD_DIM
HEAD_ROWS = 16
QK_DEPTH = MXU_DIM
V_ROWS = HEAD_DIM + 16
SCAN_CHUNK = 256
PAGES_PER_STEP = 8
VMEM_LIMIT = 56 * 1024 * 1024
LOG2E = math.log2(math.e)
NEG = -1e30

_NT = (((1,), (1,)), ((), ()))


def _bf16(x):
    return x.astype(jnp.bfloat16)


def _dot(a, b):
    return jnp.dot(a, b, preferred_element_type=jnp.float32)


def _split3(x):
    hi = _bf16(x)
    r1 = x - hi.astype(jnp.float32)
    mid = _bf16(r1)
    lo = _bf16(r1 - mid.astype(jnp.float32))
    return hi, mid, lo


def _log_sigmoid(x):
    return jnp.minimum(x, 0.0) - jnp.log1p(jnp.exp(-jnp.abs(x)))


def _sigmoid(x):
    return 1.0 / (1.0 + jnp.exp(-x))


def _rms(x, g):
    ms = jnp.mean(x * x, axis=-1, keepdims=True)
    return x * lax.rsqrt(ms + EPS) * g


def _params(sem):
    return pltpu.CompilerParams(dimension_semantics=sem, vmem_limit_bytes=VMEM_LIMIT)


def _inproj_prompt_kernel(x_ref, g_ref, w_ref, wt_ref, wf_ref, bf_ref, tril_ref, psel_ref,
                          ag_ref, k_ref, v_ref, qat_ref, ka_ref, vat_ref, lf_ref,
                          xn_scr, aug_scr, carry_scr, *, tm, tn, q_scale):
    i = pl.program_id(0)
    j = pl.program_id(1)
    hpt = tn // HEAD_DIM
    tps = ATTN_WIDTH // tn

    @pl.when(j == 0)
    def _():
        xn_scr[...] = _bf16(_rms(x_ref[...], g_ref[...]))
        lf = _log_sigmoid(_dot(xn_scr[...], wf_ref[...]) + bf_ref[...])
        lf_ref[...] = lf

        @pl.when(i == 0)
        def _():
            carry_scr[...] = jnp.zeros_like(carry_scr)

        carry = carry_scr[0:1, :]
        tril = tril_ref[...]
        for c in range(tm // SCAN_CHUNK):
            rows = slice(c * SCAN_CHUNK, (c + 1) * SCAN_CHUNK)
            hi, mid, lo = _split3(lf[rows, :])
            cs = _dot(tril, hi) + _dot(tril, mid) + _dot(tril, lo) + carry
            h2, m2, l2 = _split3(cs * (-LOG2E))
            aug = (_dot(h2, psel_ref[0:LANE, :]) + _dot(m2, psel_ref[LANE:2 * LANE, :])
                   + _dot(l2, psel_ref[2 * LANE:3 * LANE, :]))
            aug_scr[rows, :] = _bf16(aug)
            carry = cs[SCAN_CHUNK - 1:SCAN_CHUNK, :]
        carry_scr[...] = jnp.broadcast_to(carry, carry_scr.shape)

    def nn():
        return _dot(xn_scr[...], w_ref[...])

    def nt():
        return lax.dot_general(wt_ref[...], xn_scr[...], _NT,
                               preferred_element_type=jnp.float32)

    @pl.when(j < 2 * tps)
    def _():
        z = nn()
        ag_ref[...] = z[:, :tn // 2] * _sigmoid(z[:, tn // 2:])

    for t in range(tps):
        @pl.when(j == 2 * tps + t)
        def _(t=t):
            zt = nt() * q_scale
            ones = (lax.broadcasted_iota(jnp.int32, (QK_DEPTH - HEAD_DIM, tm), 0) < 3
                    ).astype(jnp.bfloat16)
            for hh in range(hpt):
                r = (t * hpt + hh) * QK_DEPTH
                qat_ref[r:r + HEAD_DIM, :] = _bf16(zt[hh * HEAD_DIM:(hh + 1) * HEAD_DIM, :])
                qat_ref[r + HEAD_DIM:r + QK_DEPTH, :] = ones

        @pl.when(j == 3 * tps + t)
        def _(t=t):
            z = nn()
            for hh in range(hpt):
                h = t * hpt + hh
                c = h * QK_DEPTH
                zh = z[:, hh * HEAD_DIM:(hh + 1) * HEAD_DIM]
                k_ref[:, h, :] = zh
                ka_ref[:, c:c + HEAD_DIM] = _bf16(zh)
                ka_ref[:, c + HEAD_DIM:c + QK_DEPTH] = aug_scr[:, h * LANE:(h + 1) * LANE]

        @pl.when(j == 4 * tps + t)
        def _(t=t):
            z = nn()
            for hh in range(hpt):
                v_ref[:, t * hpt + hh, :] = z[:, hh * HEAD_DIM:(hh + 1) * HEAD_DIM]
            zt = nt()
            ones = (lax.broadcasted_iota(jnp.int32, (V_ROWS - HEAD_DIM, tm), 0) == 0
                    ).astype(jnp.bfloat16)
            for hh in range(hpt):
                r = (t * hpt + hh) * V_ROWS
                vat_ref[r:r + HEAD_DIM, :] = _bf16(zt[hh * HEAD_DIM:(hh + 1) * HEAD_DIM, :])
                vat_ref[r + HEAD_DIM:r + V_ROWS, :] = ones


def _inproj_prompt(x, g, w_main, wt_qv, wf_pad, bf_row, tril, psel, *, tm, tn):
    m, d = x.shape
    tps = ATTN_WIDTH // tn
    nj = 5 * tps
    assert w_main.shape == (d, 5 * ATTN_WIDTH) and m % tm == 0 and tm % SCAN_CHUNK == 0
    kern = functools.partial(_inproj_prompt_kernel, tm=tm, tn=tn,
                             q_scale=HEAD_DIM ** -0.5 * LOG2E)

    def w_map(i, j):
        return (0, jnp.where((j >= 2 * tps) & (j < 3 * tps), 2 * tps - 1, j))

    def wt_map(i, j):
        t = jnp.clip(j - 2 * tps, 0, tps - 1) + jnp.where(j >= 3 * tps, 1, 0) \
            + jnp.clip(j - 4 * tps, 0, tps - 1)
        return (jnp.minimum(t, 2 * tps - 1), 0)

    return pl.pallas_call(
        kern,
        grid=(m // tm, nj),
        in_specs=[
            pl.BlockSpec((tm, d), lambda i, j: (i, 0)),
            pl.BlockSpec((1, d), lambda i, j: (0, 0)),
            pl.BlockSpec((d, tn), w_map),
            pl.BlockSpec((tn, d), wt_map),
            pl.BlockSpec((d, LANE), lambda i, j: (0, 0)),
            pl.BlockSpec((1, LANE), lambda i, j: (0, 0)),
            pl.BlockSpec((SCAN_CHUNK, SCAN_CHUNK), lambda i, j: (0, 0)),
            pl.BlockSpec((3 * LANE, ATTN_WIDTH), lambda i, j: (0, 0)),
        ],
        out_specs=[
            pl.BlockSpec((tm, tn // 2), lambda i, j: (i, jnp.minimum(j, 2 * tps - 1))),
            pl.BlockSpec((tm, N_HEADS, HEAD_DIM), lambda i, j: (i, 0, 0)),
            pl.BlockSpec((tm, N_HEADS, HEAD_DIM), lambda i, j: (i, 0, 0)),
            pl.BlockSpec((None, N_HEADS * QK_DEPTH, tm), lambda i, j: (i, 0, 0)),
            pl.BlockSpec((tm, N_HEADS * QK_DEPTH), lambda i, j: (i, 0)),
            pl.BlockSpec((None, N_HEADS * V_ROWS, tm), lambda i, j: (i, 0, 0)),
            pl.BlockSpec((tm, LANE), lambda i, j: (i, 0)),
        ],
        out_shape=[
            jax.ShapeDtypeStruct((m, ATTN_WIDTH), jnp.float32),
            jax.ShapeDtypeStruct((m, N_HEADS, HEAD_DIM), jnp.float32),
            jax.ShapeDtypeStruct((m, N_HEADS, HEAD_DIM), jnp.float32),
            jax.ShapeDtypeStruct((m // tm, N_HEADS * QK_DEPTH, tm), jnp.bfloat16),
            jax.ShapeDtypeStruct((m, N_HEADS * QK_DEPTH), jnp.bfloat16),
            jax.ShapeDtypeStruct((m // tm, N_HEADS * V_ROWS, tm), jnp.bfloat16),
            jax.ShapeDtypeStruct((m, LANE), jnp.float32),
        ],
        scratch_shapes=[pltpu.VMEM((tm, d), jnp.bfloat16),
                        pltpu.VMEM((tm, ATTN_WIDTH), jnp.bfloat16),
                        pltpu.VMEM((SUBLANE, LANE), jnp.float32)],
        compiler_params=_params(("arbitrary", "arbitrary")),
        name="inproj_prompt",
    )(x, g, w_main, wt_qv, wf_pad, bf_row, tril, psel)


def _inproj_sample_kernel(x_ref, g_ref, w_ref, wf_ref, bf_ref,
                          ag_ref, kv_ref, q_ref, lf_ref, xn_scr, *, q_scale):
    j = pl.program_id(0)

    @pl.when(j == 0)
    def _():
        xn_scr[...] = _bf16(_rms(x_ref[...], g_ref[...]))
        lf_ref[...] = _log_sigmoid(_dot(xn_scr[...], wf_ref[...]) + bf_ref[...])

    z = _dot(xn_scr[...], w_ref[...])

    @pl.when(j < 2)
    def _():
        ag_ref[...] = z

    @pl.when(j == 2)
    def _():
        q_ref[...] = _bf16(z * q_scale)

    @pl.when(j > 2)
    def _():
        kv_ref[...] = z


def _inproj_sample(x, g, w_main, wf_pad, bf_row):
    m, d = x.shape
    tn = ATTN_WIDTH
    kern = functools.partial(_inproj_sample_kernel, q_scale=HEAD_DIM ** -0.5)
    return pl.pallas_call(
        kern,
        grid=(5,),
        in_specs=[
            pl.BlockSpec((m, d), lambda j: (0, 0)),
            pl.BlockSpec((1, d), lambda j: (0, 0)),
            pl.BlockSpec((d, tn), lambda j: (0, j)),
            pl.BlockSpec((d, LANE), lambda j: (0, 0)),
            pl.BlockSpec((1, LANE), lambda j: (0, 0)),
        ],
        out_specs=[
            pl.BlockSpec((m, tn), lambda j: (0, jnp.minimum(j, 1))),
            pl.BlockSpec((m, tn), lambda j: (0, jnp.clip(j - 3, 0, 1))),
            pl.BlockSpec((m, tn), lambda j: (0, 0)),
            pl.BlockSpec((m, LANE), lambda j: (0, 0)),
        ],
        out_shape=[
            jax.ShapeDtypeStruct((m, 2 * tn), jnp.float32),
            jax.ShapeDtypeStruct((m, 2 * tn), jnp.float32),
            jax.ShapeDtypeStruct((m, tn), jnp.bfloat16),
            jax.ShapeDtypeStruct((m, LANE), jnp.float32),
        ],
        scratch_shapes=[pltpu.VMEM((m, d), jnp.bfloat16)],
        compiler_params=_params(("arbitrary",)),
        name="inproj_sample",
    )(x, g, w_main, wf_pad, bf_row)


def _layernorm_swish(y, g, b):
    mu = jnp.mean(y, axis=-1, keepdims=True)
    yc = y - mu
    var = jnp.mean(yc * yc, axis=-1, keepdims=True)
    yn = yc * lax.rsqrt(var + EPS) * g + b
    return yn * _sigmoid(yn)


def _conv_prompt_kernel(u_ref, w_ref, b_ref, lng_ref, lnb_ref,
                        out_ref, tail_ref, uext_scr, ush_scr, y_scr, *, tb, kw, hist):
    i = pl.program_id(0)

    @pl.when(i == 0)
    def _():
        uext_scr[0:hist, :] = jnp.zeros((hist, uext_scr.shape[1]), jnp.float32)

    uext_scr[hist:hist + tb, :] = u_ref[...]
    span = ush_scr.shape[1]
    for sh in range(1, SUBLANE):
        ush_scr[sh - 1] = uext_scr[sh:sh + span, :]
    ch = u_ref.shape[1]
    rows = 128
    off = hist - (kw - 1)
    nrc = tb // rows
    def lane_chunk(cc, carry):
        cs = pl.ds(pl.multiple_of(cc * LANE, LANE), LANE)
        accs = [jnp.broadcast_to(b_ref[:, cs], (rows, LANE))] * nrc
        for t in range(kw):
            sh = (off + t) % SUBLANE
            wt = w_ref[t:t + 1, cs]
            for rc in range(nrc):
                r0 = rc * rows + (off + t) - sh
                if sh == 0:
                    u = uext_scr[r0:r0 + rows, cs]
                else:
                    u = ush_scr[sh - 1, r0:r0 + rows, cs]
                accs[rc] = accs[rc] + wt * u
        for rc in range(nrc):
            y_scr[rc * rows:(rc + 1) * rows, cs] = accs[rc]
        return carry

    lax.fori_loop(0, ch // LANE, lane_chunk, 0)
    out_ref[...] = _bf16(_layernorm_swish(y_scr[...], lng_ref[...], lnb_ref[...]))
    tail = uext_scr[tb:tb + hist, :]
    uext_scr[0:hist, :] = tail
    tail_ref[...] = tail


def _conv_prompt(u, w, b, lng, lnb, *, tb):
    s, ch = u.shape
    kw = w.shape[0]
    hist = 32
    assert kw - 1 <= hist and s % tb == 0
    kern = functools.partial(_conv_prompt_kernel, tb=tb, kw=kw, hist=hist)
    return pl.pallas_call(
        kern,
        grid=(s // tb,),
        in_specs=[
            pl.BlockSpec((tb, ch), lambda i: (i, 0)),
            pl.BlockSpec((kw, ch), lambda i: (0, 0)),
            pl.BlockSpec((1, ch), lambda i: (0, 0)),
            pl.BlockSpec((1, ch), lambda i: (0, 0)),
            pl.BlockSpec((1, ch), lambda i: (0, 0)),
        ],
        out_specs=[
            pl.BlockSpec((tb, ch), lambda i: (i, 0)),
            pl.BlockSpec((hist, ch), lambda i: (0, 0)),
        ],
        out_shape=[
            jax.ShapeDtypeStruct((s, ch), jnp.bfloat16),
            jax.ShapeDtypeStruct((hist, ch), jnp.float32),
        ],
        scratch_shapes=[pltpu.VMEM((hist + tb, ch), jnp.float32),
                        pltpu.VMEM((SUBLANE - 1, hist + tb - SUBLANE, ch), jnp.float32),
                        pltpu.VMEM((tb, ch), jnp.float32)],
        compiler_params=_params(("arbitrary",)),
        name="conv_prompt",
    )(u, w, b, lng, lnb)


def _conv_sample_kernel(a_ref, gate_ref, st_ref, w_ref, b_ref, lng_ref, lnb_ref,
                        out_ref, u_ref, *, kw):
    u = a_ref[...] * _sigmoid(gate_ref[...])
    u_ref[...] = u
    acc = b_ref[...] + w_ref[kw - 1:kw, :] * u
    for t in range(kw - 1):
        acc = acc + w_ref[t:t + 1, :] * st_ref[t]
    out_ref[...] = _bf16(_layernorm_swish(acc, lng_ref[...], lnb_ref[...]))


def _conv_sample(ag, state_t, w, b, lng, lnb):
    nb = ag.shape[0]
    ch = ag.shape[1] // 2
    kw = w.shape[0]
    kern = functools.partial(_conv_sample_kernel, kw=kw)
    return pl.pallas_call(
        kern,
        grid=(1,),
        in_specs=[
            pl.BlockSpec((nb, ch), lambda i: (0, 0)),
            pl.BlockSpec((nb, ch), lambda i: (0, 1)),
            pl.BlockSpec((kw - 1, nb, ch), lambda i: (0, 0, 0)),
            pl.BlockSpec((kw, ch), lambda i: (0, 0)),
            pl.BlockSpec((1, ch), lambda i: (0, 0)),
            pl.BlockSpec((1, ch), lambda i: (0, 0)),
            pl.BlockSpec((1, ch), lambda i: (0, 0)),
        ],
        out_specs=[
            pl.BlockSpec((nb, ch), lambda i: (0, 0)),
            pl.BlockSpec((nb, ch), lambda i: (0, 0)),
        ],
        out_shape=[
            jax.ShapeDtypeStruct((nb, ch), jnp.bfloat16),
            jax.ShapeDtypeStruct((nb, ch), jnp.float32),
        ],
        compiler_params=_params(("arbitrary",)),
        name="conv_sample",
    )(ag, ag, state_t, w, b, lng, lnb)


def _flash_kernel(qat_ref, ka_ref, vat_ref, o_ref, s_scr, acc_scr, *, tk):
    i = pl.program_id(1)
    tq = 2 * tk
    qat = jnp.concatenate([qat_ref[0], qat_ref[1]], axis=1)
    acc_scr[...] = jnp.zeros_like(acc_scr)

    def block_of(k):
        return jnp.where(k < 2, 2 * i + k, k - 2)

    def scores(k, slot, diag=None):
        j = k - 2 if diag is None else 2 * i + diag
        r0 = pl.multiple_of(j * tk, tk)
        st = _dot(ka_ref[pl.ds(r0, tk), :], qat)
        if diag is not None:
            key = lax.broadcasted_iota(jnp.int32, (tk, tq), 0) + diag * tk
            qry = lax.broadcasted_iota(jnp.int32, (tk, tq), 1)
            st = jnp.where(key <= qry, st, NEG)
        s_scr[slot] = st
        return jnp.max(st, axis=0, keepdims=True)

    def accumulate(k, slot, m_old, mx):
        m_new = jnp.maximum(m_old, mx)
        alpha = jnp.exp2(m_old - m_new)
        pt = _bf16(jnp.exp2(s_scr[slot] - m_new))
        acc_scr[...] = alpha * acc_scr[...] + _dot(vat_ref[block_of(k)], pt)
        return m_new

    def body(p, carry):
        m, mx0 = carry
        k = 2 * p
        mx1 = scores(k + 1, 1)
        m = accumulate(k, 0, m, mx0)
        mx0 = scores(k + 2, 0)
        m = accumulate(k + 1, 1, m, mx1)
        return m, mx0

    m0 = jnp.full((1, tq), NEG, jnp.float32)
    last = 2 * i + 1

    @pl.when(i == 0)
    def _():
        mx0 = scores(0, 0, diag=0)
        mx1 = scores(1, 1, diag=1)
        m = accumulate(0, 0, m0, mx0)
        accumulate(1, 1, m, mx1)

    @pl.when(i > 0)
    def _():
        mx0 = scores(0, 0, diag=0)
        mx1 = scores(1, 1, diag=1)
        m = accumulate(0, 0, m0, mx0)
        mx0 = scores(2, 0)
        m = accumulate(1, 1, m, mx1)
        m, mx0 = lax.fori_loop(1, i, body, (m, mx0))
        mx1 = scores(last, 1)
        m = accumulate(last - 1, 0, m, mx0)
        accumulate(last, 1, m, mx1)

    acc = acc_scr[...]
    out_t = acc[0:HEAD_DIM, :] / acc[HEAD_DIM:HEAD_DIM + 1, :]
    o_ref[...] = _bf16(jnp.transpose(out_t))


def _flash_prompt(qat, ka, vat):
    s = ka.shape[0]
    nblk, _, tk = qat.shape
    tq = 2 * tk
    assert nblk * tk == s and s % tq == 0 and vat.shape == (nblk, N_HEADS * V_ROWS, tk)
    kern = functools.partial(_flash_kernel, tk=tk)
    return pl.pallas_call(
        kern,
        grid=(N_HEADS, s // tq),
        in_specs=[
            pl.BlockSpec((2, QK_DEPTH, tk), lambda h, i: (i, h, 0)),
            pl.BlockSpec((s, QK_DEPTH), lambda h, i: (0, h)),
            pl.BlockSpec((nblk, V_ROWS, tk), lambda h, i: (0, h, 0)),
        ],
        out_specs=pl.BlockSpec((tq, HEAD_DIM), lambda h, i: (i, h)),
        out_shape=jax.ShapeDtypeStruct((s, ATTN_WIDTH), jnp.bfloat16),
        scratch_shapes=[pltpu.VMEM((2, tk, tq), jnp.float32),
                        pltpu.VMEM((V_ROWS, tq), jnp.float32)],
        compiler_params=_params(("parallel", "arbitrary")),
        name="flash_prompt",
    )(qat, ka, vat)


def _decode_kernel(pt_ref, q_ref, kn_ref, vn_ref, lfn_ref, *refs, npp):
    k_refs = refs[0:npp]
    v_refs = refs[npp:2 * npp]
    lf_refs = refs[2 * npp:3 * npp]
    o_ref = refs[3 * npp]
    m_scr, l_scr, acc_scr, run_scr, lf_scr = refs[3 * npp + 1:]
    g = pl.program_id(1)
    rows = q_ref.shape[0]
    width = k_refs[0].shape[0]
    nch = width // LANE
    q = q_ref[...]

    @pl.when(g == 0)
    def _():
        kn = _bf16(kn_ref[...]).astype(jnp.float32)
        s_new = jnp.sum(q.astype(jnp.float32) * kn, axis=-1, keepdims=True)
        m_scr[...] = s_new
        l_scr[...] = jnp.ones_like(l_scr)
        acc_scr[...] = _bf16(vn_ref[...]).astype(jnp.float32)
        run_scr[...] = lfn_ref[...]

    for p in range(npp):
        lf_scr[p:p + 1, :] = lf_refs[p][...]
    x = lf_scr[...]
    xs = jnp.concatenate([x[:, c * LANE:(c + 1) * LANE] for c in range(nch)], axis=0)
    lane = lax.broadcasted_iota(jnp.int32, xs.shape, 1)
    y = xs
    tot = xs
    for sh in (8, 16, 32, 64):
        y = y + jnp.where(lane + sh < LANE, pltpu.roll(y, LANE - sh, axis=1), 0.0)
        tot = tot + pltpu.roll(tot, sh, axis=1)
    excl = y - xs
    later = jnp.zeros((npp, LANE), jnp.float32)
    pieces = [None] * nch
    for c in reversed(range(nch)):
        pieces[c] = excl[c * npp:(c + 1) * npp, :] + later
        later = later + tot[c * npp:(c + 1) * npp, :]
    within = jnp.concatenate(pieces, axis=1)
    page_tot = later

    hrow = lax.broadcasted_iota(jnp.int32, (rows, width), 0)
    hcol = lax.broadcasted_iota(jnp.int32, (rows, width), 1) & (N_HEADS - 1)
    own = hrow == hcol
    run = run_scr[...]
    scores = [None] * npp
    for p in reversed(range(npp)):
        bias = within[p:p + 1, :] + jnp.concatenate([run] * nch, axis=1)
        s = lax.dot_general(q, _bf16(k_refs[p][...]), _NT, preferred_element_type=jnp.float32)
        scores[p] = jnp.where(own, s + bias, NEG)
        run = run + page_tot[p:p + 1, :]
    run_scr[...] = run
    smax = scores[0]
    for p in range(1, npp):
        smax = jnp.maximum(smax, scores[p])
    m_old = m_scr[...]
    m_new = jnp.maximum(m_old, jnp.max(smax, axis=-1, keepdims=True))
    alpha = jnp.exp(m_old - m_new)
    psum = jnp.zeros((rows, width), jnp.float32)
    acc = alpha * acc_scr[...]
    for p in range(npp):
        pr = jnp.exp(scores[p] - m_new)
        psum = psum + pr
        acc = acc + _dot(_bf16(pr), _bf16(v_refs[p][...]))
    l_new = alpha * l_scr[...] + jnp.sum(psum, axis=-1, keepdims=True)
    m_scr[...] = m_new
    l_scr[...] = l_new
    acc_scr[...] = acc

    @pl.when(g == pl.num_programs(1) - 1)
    def _():
        o_ref[...] = acc / l_new


def _decode_attention(page_table, q16, k_new16, v_new16, lf_row, kc, vc, lfc):
    nb, n_pages = page_table.shape
    npp = PAGES_PER_STEP
    assert n_pages % npp == 0
    ng = n_pages // npp
    width = kc.shape[1]
    rows = q16.shape[1]

    def page_map(p):
        return lambda b, g, pt: (pt[b, (ng - 1 - g) * npp + p], 0, 0)

    per_b = lambda b, g, pt: (b, 0, 0)
    in_specs = [pl.BlockSpec((None, rows, HEAD_DIM), per_b)] * 3
    in_specs.append(pl.BlockSpec((None, 1, LANE), per_b))
    in_specs += [pl.BlockSpec((None, width, HEAD_DIM), page_map(p % npp))
                 for p in range(2 * npp)]
    in_specs += [pl.BlockSpec((None, 1, width), page_map(p)) for p in range(npp)]
    kern = functools.partial(_decode_kernel, npp=npp)
    grid_spec = pltpu.PrefetchScalarGridSpec(
        num_scalar_prefetch=1,
        grid=(nb, ng),
        in_specs=in_specs,
        out_specs=pl.BlockSpec((None, rows, HEAD_DIM), per_b),
        scratch_shapes=[pltpu.VMEM((rows, 1), jnp.float32),
                        pltpu.VMEM((rows, 1), jnp.float32),
                        pltpu.VMEM((rows, HEAD_DIM), jnp.float32),
                        pltpu.VMEM((1, LANE), jnp.float32),
                        pltpu.VMEM((npp, width), jnp.float32)],
    )
    return pl.pallas_call(
        kern,
        grid_spec=grid_spec,
        out_shape=jax.ShapeDtypeStruct((nb, rows, HEAD_DIM), jnp.float32),
        compiler_params=_params(("arbitrary", "arbitrary")),
        name="decode_attention",
    )(page_table, q16, k_new16, v_new16, lf_row, *([kc] * npp), *([vc] * npp), *([lfc] * npp))


def _outproj_kernel(c_ref, a_ref, x_ref, w1_ref, w2_ref, g_ref, h_ref, hn_ref):
    h = x_ref[...] + _dot(c_ref[...], w1_ref[...]) + _dot(a_ref[...], w2_ref[...])
    h_ref[...] = h
    hn_ref[...] = _bf16(_rms(h, g_ref[...]))


def _outproj(conv_out, attn_out, x, w_out, g, *, tm):
    m, d = x.shape
    half = conv_out.shape[1]
    return pl.pallas_call(
        _outproj_kernel,
        grid=(m // tm,),
        in_specs=[
            pl.BlockSpec((tm, half), lambda i: (i, 0)),
            pl.BlockSpec((tm, half), lambda i: (i, 0)),
            pl.BlockSpec((tm, d), lambda i: (i, 0)),
            pl.BlockSpec((half, d), lambda i: (0, 0)),
            pl.BlockSpec((half, d), lambda i: (1, 0)),
            pl.BlockSpec((1, d), lambda i: (0, 0)),
        ],
        out_specs=[
            pl.BlockSpec((tm, d), lambda i: (i, 0)),
            pl.BlockSpec((tm, d), lambda i: (i, 0)),
        ],
        out_shape=[
            jax.ShapeDtypeStruct((m, d), jnp.float32),
            jax.ShapeDtypeStruct((m, d), jnp.bfloat16),
        ],
        compiler_params=_params(("parallel",)),
        name="outproj",
    )(conv_out, attn_out, x, w_out, w_out, g)


def _ffn_prompt_kernel(hn_ref, h_ref, wa_ref, wb_ref, wd_ref, cwa_ref, cwb_ref,
                       cba_ref, cbb_ref, gf_ref,
                       y_ref, ta_ref, tb_ref, acc_scr, up_scr, tail_scr, *, tm, tf):
    i = pl.program_id(0)
    j = pl.program_id(1)
    nj = pl.num_programs(1)
    hist = SUBLANE
    slot = pl.multiple_of(j * hist, hist)

    @pl.when(j == 0)
    def _():
        acc_scr[...] = jnp.zeros_like(acc_scr)

    @pl.when(i == 0)
    def _():
        up_scr[0:hist, :] = jnp.zeros((hist, 2 * tf), jnp.float32)

    @pl.when(i > 0)
    def _():
        up_scr[0:hist, :] = tail_scr[pl.ds(slot, hist), :]

    hn = hn_ref[...]
    up_scr[hist:hist + tm, 0:tf] = _dot(hn, wa_ref[...])
    up_scr[hist:hist + tm, tf:2 * tf] = _dot(hn, wb_ref[...])

    def conv(cols, cw_ref, cb_ref):
        return (cb_ref[...]
                + cw_ref[0:1, :] * up_scr[hist - 2:hist - 2 + tm, cols]
                + cw_ref[1:2, :] * up_scr[hist - 1:hist - 1 + tm, cols]
                + cw_ref[2:3, :] * up_scr[hist:hist + tm, cols])

    ca = conv(slice(0, tf), cwa_ref, cba_ref)
    cb = conv(slice(tf, 2 * tf), cwb_ref, cbb_ref)
    act = _bf16(ca * _sigmoid(ca) * cb)
    acc_scr[...] += _dot(act, wd_ref[...])

    tail = up_scr[tm:tm + hist, :]
    tail_scr[pl.ds(slot, hist), :] = tail

    @pl.when(i == pl.num_programs(0) - 1)
    def _():
        cols = pl.ds(pl.multiple_of(j * tf, tf), tf)
        ta_ref[:, cols] = tail[:, 0:tf]
        tb_ref[:, cols] = tail[:, tf:2 * tf]

    @pl.when(j == nj - 1)
    def _():
        y_ref[...] = _rms(h_ref[...] + acc_scr[...], gf_ref[...])


def _ffn_prompt(hn, h, w_up, w_down, cw, cb, gf, *, tm, tf):
    m, d = h.shape
    dff = w_down.shape[0]
    assert dff % tf == 0 and m % tm == 0
    nj = dff // tf
    kern = functools.partial(_ffn_prompt_kernel, tm=tm, tf=tf)
    return pl.pallas_call(
        kern,
        grid=(m // tm, nj),
        in_specs=[
            pl.BlockSpec((tm, d), lambda i, j: (i, 0)),
            pl.BlockSpec((tm, d), lambda i, j: (i, 0)),
            pl.BlockSpec((d, tf), lambda i, j: (0, j)),
            pl.BlockSpec((d, tf), lambda i, j: (0, j + nj)),
            pl.BlockSpec((tf, d), lambda i, j: (j, 0)),
            pl.BlockSpec((3, tf), lambda i, j: (0, j)),
            pl.BlockSpec((3, tf), lambda i, j: (0, j + nj)),
            pl.BlockSpec((1, tf), lambda i, j: (0, j)),
            pl.BlockSpec((1, tf), lambda i, j: (0, j + nj)),
            pl.BlockSpec((1, d), lambda i, j: (0, 0)),
        ],
        out_specs=[
            pl.BlockSpec((tm, d), lambda i, j: (i, 0)),
            pl.BlockSpec((SUBLANE, dff), lambda i, j: (0, 0)),
            pl.BlockSpec((SUBLANE, dff), lambda i, j: (0, 0)),
        ],
        out_shape=[
            jax.ShapeDtypeStruct((m, d), jnp.float32),
            jax.ShapeDtypeStruct((SUBLANE, dff), jnp.float32),
            jax.ShapeDtypeStruct((SUBLANE, dff), jnp.float32),
        ],
        scratch_shapes=[pltpu.VMEM((tm, d), jnp.float32),
                        pltpu.VMEM((SUBLANE + tm, 2 * tf), jnp.float32),
                        pltpu.VMEM((nj * SUBLANE, 2 * tf), jnp.float32)],
        compiler_params=_params(("arbitrary", "arbitrary")),
        name="ffn_prompt",
    )(hn, h, w_up, w_up, w_down, cw, cw, cb, cb, gf)


def _ffn_sample_kernel(hn_ref, h_ref, wa_ref, wb_ref, wd_ref, sta_ref, stb_ref,
                       cwa_ref, cwb_ref, cba_ref, cbb_ref, gf_ref,
                       y_ref, ua_ref, ub_ref, acc_scr):
    j = pl.program_id(0)

    @pl.when(j == 0)
    def _():
        acc_scr[...] = jnp.zeros_like(acc_scr)

    hn = hn_ref[...]
    ua = _dot(hn, wa_ref[...])
    ub = _dot(hn, wb_ref[...])
    ua_ref[...] = ua
    ub_ref[...] = ub
    ca = cba_ref[...] + cwa_ref[0:1, :] * sta_ref[0] + cwa_ref[1:2, :] * sta_ref[1] + cwa_ref[2:3, :] * ua
    cb = cbb_ref[...] + cwb_ref[0:1, :] * stb_ref[0] + cwb_ref[1:2, :] * stb_ref[1] + cwb_ref[2:3, :] * ub
    act = _bf16(ca * _sigmoid(ca) * cb)
    acc_scr[...] += _dot(act, wd_ref[...])

    @pl.when(j == pl.num_programs(0) - 1)
    def _():
        y_ref[...] = _rms(h_ref[...] + acc_scr[...], gf_ref[...])


def _ffn_sample(hn, h, w_up, w_down, state_t, cw, cb, gf, *, tf):
    m, d = h.shape
    dff = w_down.shape[0]
    nj = dff // tf
    return pl.pallas_call(
        _ffn_sample_kernel,
        grid=(nj,),
        in_specs=[
            pl.BlockSpec((m, d), lambda j: (0, 0)),
            pl.BlockSpec((m, d), lambda j: (0, 0)),
            pl.BlockSpec((d, tf), lambda j: (0, j)),
            pl.BlockSpec((d, tf), lambda j: (0, j + nj)),
            pl.BlockSpec((tf, d), lambda j: (j, 0)),
            pl.BlockSpec((2, m, tf), lambda j: (0, 0, j)),
            pl.BlockSpec((2, m, tf), lambda j: (0, 0, j + nj)),
            pl.BlockSpec((3, tf), lambda j: (0, j)),
            pl.BlockSpec((3, tf), lambda j: (0, j + nj)),
            pl.BlockSpec((1, tf), lambda j: (0, j)),
            pl.BlockSpec((1, tf), lambda j: (0, j + nj)),
            pl.BlockSpec((1, d), lambda j: (0, 0)),
        ],
        out_specs=[
            pl.BlockSpec((m, d), lambda j: (0, 0)),
            pl.BlockSpec((m, tf), lambda j: (0, j)),
            pl.BlockSpec((m, tf), lambda j: (0, j)),
        ],
        out_shape=[
            jax.ShapeDtypeStruct((m, d), jnp.float32),
            jax.ShapeDtypeStruct((m, dff), jnp.float32),
            jax.ShapeDtypeStruct((m, dff), jnp.float32),
        ],
        scratch_shapes=[pltpu.VMEM((m, d), jnp.float32)],
        compiler_params=_params(("arbitrary",)),
        name="ffn_sample",
    )(hn, h, w_up, w_up, w_down, state_t, state_t, cw, cw, cb, cb, gf)


def _pick(n, prefs):
    for p in prefs:
        if n % p == 0:
            return p
    return n


def kernel(x_prompt, x_sample, cache_k, cache_v, cache_logf, state_conv, state_ffn, page_table,
           norm_mix_g, w_in, b_f, conv_dw_w, conv_dw_b, conv_ln_g, conv_ln_b, w_out,
           norm_ffn_g, w_up, ffn_dw_w, ffn_dw_b, w_down, norm_final_g):
    f32 = jnp.float32
    depth = w_in.shape[0]
    assert depth == 1 and x_prompt.shape[0] == 1 and x_sample.shape[1] == 1
    s, d = x_prompt.shape[1], x_prompt.shape[2]
    nb = x_sample.shape[0]
    n_pool, page = cache_k.shape[1], cache_k.shape[2]
    assert cache_k.shape[3:] == (N_HEADS, HEAD_DIM)
    dff = w_down.shape[1]
    conv_ch = conv_dw_w.shape[2]
    assert conv_ch == ATTN_WIDTH
    n_main = 2 * conv_ch + 3 * ATTN_WIDTH

    w_in_bf = _bf16(w_in[0])
    q0, v0 = 2 * conv_ch, 2 * conv_ch + 2 * ATTN_WIDTH
    wt_qv = jnp.transpose(jnp.concatenate(
        [w_in_bf[:, q0:q0 + ATTN_WIDTH], w_in_bf[:, v0:v0 + ATTN_WIDTH]], axis=1))
    tn_in = 512
    glu = tn_in // 2
    w_ag = jnp.stack([w_in_bf[:, :conv_ch].reshape(d, conv_ch // glu, glu),
                      w_in_bf[:, conv_ch:q0].reshape(d, conv_ch // glu, glu)], axis=2)
    w_main = jnp.concatenate([w_ag.reshape(d, q0), w_in_bf[:, q0:n_main]], axis=1)
    wf_pad = jnp.pad(w_in_bf[:, n_main:], ((0, 0), (0, LANE - N_HEADS)))
    bf_row = jnp.pad(b_f[0].reshape(1, N_HEADS), ((0, 0), (0, LANE - N_HEADS)))
    w_out_bf = _bf16(w_out[0])
    w_down_bf = _bf16(w_down[0])
    w_up_bf = _bf16(w_up[0])
    tf = _pick(dff, (512, 256))
    row = lambda v: v.reshape(1, -1)
    g_mix, g_ffn, g_fin = row(norm_mix_g[0]), row(norm_ffn_g[0]), row(norm_final_g)
    cw, cb_, lng, lnb = conv_dw_w[0], row(conv_dw_b[0]), row(conv_ln_g[0]), row(conv_ln_b[0])
    fcw, fcb = ffn_dw_w[0], row(ffn_dw_b[0])
    kw = cw.shape[0]
    tril = _bf16(jnp.tril(jnp.ones((SCAN_CHUNK, SCAN_CHUNK), f32)))
    src = jnp.arange(3 * LANE)
    dst = jnp.arange(ATTN_WIDTH)
    psel = _bf16(((src[:, None] % LANE == dst[None, :] // LANE)
                  & (src[:, None] // LANE == dst[None, :] % LANE)
                  & (src[:, None] % LANE < N_HEADS)).astype(f32))

    xp = x_prompt[0]
    ag_p, k_p, v_p, qat_p, ka_p, vat_p, lf_p = _inproj_prompt(
        xp, g_mix, w_main, wt_qv, wf_pad, bf_row, tril, psel, tm=_pick(s, (512, 256)), tn=tn_in)
    conv_p, utail_p = _conv_prompt(ag_p, cw, cb_, lng, lnb, tb=_pick(s, (256, 128)))
    attn_p = _flash_prompt(qat_p, ka_p, vat_p)
    h_p, hn_p = _outproj(conv_p, attn_p, xp, w_out_bf, g_ffn, tm=_pick(s, (512, 256)))
    y_p, ta_p, tb_p = _ffn_prompt(hn_p, h_p, w_up_bf, w_down_bf, fcw, fcb, g_fin,
                                  tm=_pick(s, (512, 256)), tf=tf)

    y_prompt = y_p[None]
    k_prompt = k_p[None, None]
    v_prompt = v_p[None, None]
    logf_prompt = lf_p[:, :N_HEADS].reshape(1, 1, s, N_HEADS)
    conv_prompt = utail_p[utail_p.shape[0] - (kw - 1):][None, None]
    ffn_prompt = jnp.concatenate([ta_p[SUBLANE - 2:], tb_p[SUBLANE - 2:]], axis=1)[None, None]

    xs = x_sample[:, 0, :]
    ag_s, kv_s, q_s, lf_s128 = _inproj_sample(xs, g_mix, w_main, wf_pad, bf_row)
    ag_s = jnp.swapaxes(ag_s.reshape(nb, conv_ch // glu, 2, glu), 1, 2).reshape(nb, q0)
    conv_s, u_s = _conv_sample(ag_s, jnp.swapaxes(state_conv[0], 0, 1), cw, cb_, lng, lnb)

    pad_heads = lambda a: jnp.pad(a.reshape(nb, N_HEADS, HEAD_DIM),
                                  ((0, 0), (0, HEAD_ROWS - N_HEADS), (0, 0)))
    q16 = pad_heads(q_s)
    kn16 = pad_heads(kv_s[:, :ATTN_WIDTH])
    vn16 = pad_heads(kv_s[:, ATTN_WIDTH:])
    lf_s = lf_s128[:, :N_HEADS]
    lf_row = jnp.tile(lf_s, (1, LANE // N_HEADS)).reshape(nb, 1, LANE)
    kc = cache_k[0].reshape(n_pool, page * N_HEADS, HEAD_DIM)
    vc = cache_v[0].reshape(n_pool, page * N_HEADS, HEAD_DIM)
    lfc = cache_logf[0].astype(f32).reshape(n_pool, 1, page * N_HEADS)
    attn_s16 = _decode_attention(page_table, q16, kn16, vn16, lf_row, kc, vc, lfc)
    attn_s = _bf16(attn_s16[:, :N_HEADS, :].reshape(nb, ATTN_WIDTH))

    h_s, hn_s = _outproj(conv_s, attn_s, xs, w_out_bf, g_ffn, tm=nb)
    y_s, ua_s, ub_s = _ffn_sample(hn_s, h_s, w_up_bf, w_down_bf,
                                  jnp.swapaxes(state_ffn[0], 0, 1), fcw, fcb, g_fin, tf=tf)

    y_sample = y_s[:, None, :]
    k_sample = kv_s[:, :ATTN_WIDTH].reshape(1, nb, 1, N_HEADS, HEAD_DIM)
    v_sample = kv_s[:, ATTN_WIDTH:].reshape(1, nb, 1, N_HEADS, HEAD_DIM)
    logf_sample = lf_s.reshape(1, nb, 1, N_HEADS)
    conv_sample = jnp.concatenate([state_conv[0][:, 1:, :], u_s[:, None, :]], axis=1)[None]
    up_s = jnp.concatenate([ua_s, ub_s], axis=1)
    ffn_sample = jnp.concatenate([state_ffn[0][:, 1:, :], up_s[:, None, :]], axis=1)[None]

    return (y_prompt, y_sample, k_prompt, v_prompt, logf_prompt, conv_prompt, ffn_prompt,
            k_sample, v_sample, logf_sample, conv_sample, ffn_sample)
```

```python
import functools
import math

import jax
import jax.numpy as jnp
from jax import lax
from jax.experimental import pallas as pl
from jax.experimental.pallas import tpu as pltpu

EPS = 1e-6
LANE = 128
SUBLANE = 8
MXU_DIM = 256
HEAD_DIM = 128
N_HEADS = 8
ATTN_WIDTH = N_HEADS * HEAD_DIM
HEAD_ROWS = 16
QK_DEPTH = MXU_DIM
V_ROWS = HEAD_DIM + 16
SCAN_CHUNK = 256
PAGES_PER_STEP = 8
VMEM_LIMIT = 56 * 1024 * 1024
LOG2E = math.log2(math.e)
NEG = -1e30

_NT = (((1,), (1,)), ((), ()))


def _bf16(x):
    return x.astype(jnp.bfloat16)


def _dot(a, b):
    return jnp.dot(a, b, preferred_element_type=jnp.float32)


def _split3(x):
    hi = _bf16(x)
    r1 = x - hi.astype(jnp.float32)
    mid = _bf16(r1)
    lo = _bf16(r1 - mid.astype(jnp.float32))
    return hi, mid, lo


def _log_sigmoid(x):
    return jnp.minimum(x, 0.0) - jnp.log1p(jnp.exp(-jnp.abs(x)))


def _sigmoid(x):
    return 1.0 / (1.0 + jnp.exp(-x))


def _rms(x, g):
    ms = jnp.mean(x * x, axis=-1, keepdims=True)
    return x * lax.rsqrt(ms + EPS) * g


def _params(sem):
    return pltpu.CompilerParams(dimension_semantics=sem, vmem_limit_bytes=VMEM_LIMIT)


def _inproj_prompt_kernel(x_ref, g_ref, w_ref, wt_ref, wf_ref, bf_ref, tril_ref, psel_ref,
                          ag_ref, k_ref, v_ref, qat_ref, ka_ref, vat_ref, lf_ref,
                          xn_scr, aug_scr, carry_scr, *, tm, tn, q_scale):
    i = pl.program_id(0)
    j = pl.program_id(1)
    hpt = tn // HEAD_DIM
    tps = ATTN_WIDTH // tn

    @pl.when(j == 0)
    def _():
        xn_scr[...] = _bf16(_rms(x_ref[...], g_ref[...]))
        lf = _log_sigmoid(_dot(xn_scr[...], wf_ref[...]) + bf_ref[...])
        lf_ref[...] = lf

        @pl.when(i == 0)
        def _():
            carry_scr[...] = jnp.zeros_like(carry_scr)

        carry = carry_scr[0:1, :]
        tril = tril_ref[...]
        for c in range(tm // SCAN_CHUNK):
            rows = slice(c * SCAN_CHUNK, (c + 1) * SCAN_CHUNK)
            hi, mid, lo = _split3(lf[rows, :])
            cs = _dot(tril, hi) + _dot(tril, mid) + _dot(tril, lo) + carry
            h2, m2, l2 = _split3(cs * (-LOG2E))
            aug = (_dot(h2, psel_ref[0:LANE, :]) + _dot(m2, psel_ref[LANE:2 * LANE, :])
                   + _dot(l2, psel_ref[2 * LANE:3 * LANE, :]))
            aug_scr[rows, :] = _bf16(aug)
            carry = cs[SCAN_CHUNK - 1:SCAN_CHUNK, :]
        carry_scr[...] = jnp.broadcast_to(carry, carry_scr.shape)

    def nn():
        return _dot(xn_scr[...], w_ref[...])

    def nt():
        return lax.dot_general(wt_ref[...], xn_scr[...], _NT,
                               preferred_element_type=jnp.float32)

    @pl.when(j < 2 * tps)
    def _():
        ag_ref[...] = nn()

    for t in range(tps):
        @pl.when(j == 2 * tps + t)
        def _(t=t):
            zt = nt() * q_scale
            ones = (lax.broadcasted_iota(jnp.int32, (QK_DEPTH - HEAD_DIM, tm), 0) < 3
                    ).astype(jnp.bfloat16)
            for hh in range(hpt):
                r = (t * hpt + hh) * QK_DEPTH
                qat_ref[r:r + HEAD_DIM, :] = _bf16(zt[hh * HEAD_DIM:(hh + 1) * HEAD_DIM, :])
                qat_ref[r + HEAD_DIM:r + QK_DEPTH, :] = ones

        @pl.when(j == 3 * tps + t)
        def _(t=t):
            z = nn()
            for hh in range(hpt):
                h = t * hpt + hh
                c = h * QK_DEPTH
                zh = z[:, hh * HEAD_DIM:(hh + 1) * HEAD_DIM]
                k_ref[:, h, :] = zh
                ka_ref[:, c:c + HEAD_DIM] = _bf16(zh)
                ka_ref[:, c + HEAD_DIM:c + QK_DEPTH] = aug_scr[:, h * LANE:(h + 1) * LANE]

        @pl.when(j == 4 * tps + t)
        def _(t=t):
            z = nn()
            for hh in range(hpt):
                v_ref[:, t * hpt + hh, :] = z[:, hh * HEAD_DIM:(hh + 1) * HEAD_DIM]
            zt = nt()
            ones = (lax.broadcasted_iota(jnp.int32, (V_ROWS - HEAD_DIM, tm), 0) == 0
                    ).astype(jnp.bfloat16)
            for hh in range(hpt):
                r = (t * hpt + hh) * V_ROWS
                vat_ref[r:r + HEAD_DIM, :] = _bf16(zt[hh * HEAD_DIM:(hh + 1) * HEAD_DIM, :])
                vat_ref[r + HEAD_DIM:r + V_ROWS, :] = ones


def _inproj_prompt(x, g, w_main, wt_qv, wf_pad, bf_row, tril, psel, *, tm, tn):
    m, d = x.shape
    tps = ATTN_WIDTH // tn
    nj = 5 * tps
    assert w_main.shape == (d, 5 * ATTN_WIDTH) and m % tm == 0 and tm % SCAN_CHUNK == 0
    kern = functools.partial(_inproj_prompt_kernel, tm=tm, tn=tn,
                             q_scale=HEAD_DIM ** -0.5 * LOG2E)

    def w_map(i, j):
        return (0, jnp.where((j >= 2 * tps) & (j < 3 * tps), 2 * tps - 1, j))

    def wt_map(i, j):
        t = jnp.clip(j - 2 * tps, 0, tps - 1) + jnp.where(j >= 3 * tps, 1, 0) \
            + jnp.clip(j - 4 * tps, 0, tps - 1)
        return (jnp.minimum(t, 2 * tps - 1), 0)

    return pl.pallas_call(
        kern,
        grid=(m // tm, nj),
        in_specs=[
            pl.BlockSpec((tm, d), lambda i, j: (i, 0)),
            pl.BlockSpec((1, d), lambda i, j: (0, 0)),
            pl.BlockSpec((d, tn), w_map),
            pl.BlockSpec((tn, d), wt_map),
            pl.BlockSpec((d, LANE), lambda i, j: (0, 0)),
            pl.BlockSpec((1, LANE), lambda i, j: (0, 0)),
            pl.BlockSpec((SCAN_CHUNK, SCAN_CHUNK), lambda i, j: (0, 0)),
            pl.BlockSpec((3 * LANE, ATTN_WIDTH), lambda i, j: (0, 0)),
        ],
        out_specs=[
            pl.BlockSpec((tm, tn), lambda i, j: (i, jnp.minimum(j, 2 * tps - 1))),
            pl.BlockSpec((tm, N_HEADS, HEAD_DIM), lambda i, j: (i, 0, 0)),
            pl.BlockSpec((tm, N_HEADS, HEAD_DIM), lambda i, j: (i, 0, 0)),
            pl.BlockSpec((None, N_HEADS * QK_DEPTH, tm), lambda i, j: (i, 0, 0)),
            pl.BlockSpec((tm, N_HEADS * QK_DEPTH), lambda i, j: (i, 0)),
            pl.BlockSpec((None, N_HEADS * V_ROWS, tm), lambda i, j: (i, 0, 0)),
            pl.BlockSpec((tm, LANE), lambda i, j: (i, 0)),
        ],
        out_shape=[
            jax.ShapeDtypeStruct((m, 2 * ATTN_WIDTH), jnp.float32),
            jax.ShapeDtypeStruct((m, N_HEADS, HEAD_DIM), jnp.float32),
            jax.ShapeDtypeStruct((m, N_HEADS, HEAD_DIM), jnp.float32),
            jax.ShapeDtypeStruct((m // tm, N_HEADS * QK_DEPTH, tm), jnp.bfloat16),
            jax.ShapeDtypeStruct((m, N_HEADS * QK_DEPTH), jnp.bfloat16),
            jax.ShapeDtypeStruct((m // tm, N_HEADS * V_ROWS, tm), jnp.bfloat16),
            jax.ShapeDtypeStruct((m, LANE), jnp.float32),
        ],
        scratch_shapes=[pltpu.VMEM((tm, d), jnp.bfloat16),
                        pltpu.VMEM((tm, ATTN_WIDTH), jnp.bfloat16),
                        pltpu.VMEM((SUBLANE, LANE), jnp.float32)],
        compiler_params=_params(("arbitrary", "arbitrary")),
        name="inproj_prompt",
    )(x, g, w_main, wt_qv, wf_pad, bf_row, tril, psel)


def _inproj_sample_kernel(x_ref, g_ref, w_ref, wf_ref, bf_ref,
                          ag_ref, kv_ref, q_ref, lf_ref, xn_scr, *, q_scale):
    j = pl.program_id(0)

    @pl.when(j == 0)
    def _():
        xn_scr[...] = _bf16(_rms(x_ref[...], g_ref[...]))
        lf_ref[...] = _log_sigmoid(_dot(xn_scr[...], wf_ref[...]) + bf_ref[...])

    z = _dot(xn_scr[...], w_ref[...])

    @pl.when(j < 2)
    def _():
        ag_ref[...] = z

    @pl.when(j == 2)
    def _():
        q_ref[...] = _bf16(z * q_scale)

    @pl.when(j > 2)
    def _():
        kv_ref[...] = z


def _inproj_sample(x, g, w_main, wf_pad, bf_row):
    m, d = x.shape
    tn = ATTN_WIDTH
    kern = functools.partial(_inproj_sample_kernel, q_scale=HEAD_DIM ** -0.5)
    return pl.pallas_call(
        kern,
        grid=(5,),
        in_specs=[
            pl.BlockSpec((m, d), lambda j: (0, 0)),
            pl.BlockSpec((1, d), lambda j: (0, 0)),
            pl.BlockSpec((d, tn), lambda j: (0, j)),
            pl.BlockSpec((d, LANE), lambda j: (0, 0)),
            pl.BlockSpec((1, LANE), lambda j: (0, 0)),
        ],
        out_specs=[
            pl.BlockSpec((m, tn), lambda j: (0, jnp.minimum(j, 1))),
            pl.BlockSpec((m, tn), lambda j: (0, jnp.clip(j - 3, 0, 1))),
            pl.BlockSpec((m, tn), lambda j: (0, 0)),
            pl.BlockSpec((m, LANE), lambda j: (0, 0)),
        ],
        out_shape=[
            jax.ShapeDtypeStruct((m, 2 * tn), jnp.float32),
            jax.ShapeDtypeStruct((m, 2 * tn), jnp.float32),
            jax.ShapeDtypeStruct((m, tn), jnp.bfloat16),
            jax.ShapeDtypeStruct((m, LANE), jnp.float32),
        ],
        scratch_shapes=[pltpu.VMEM((m, d), jnp.bfloat16)],
        compiler_params=_params(("arbitrary",)),
        name="inproj_sample",
    )(x, g, w_main, wf_pad, bf_row)


def _layernorm_swish(y, g, b):
    mu = jnp.mean(y, axis=-1, keepdims=True)
    yc = y - mu
    var = jnp.mean(yc * yc, axis=-1, keepdims=True)
    yn = yc * lax.rsqrt(var + EPS) * g + b
    return yn * _sigmoid(yn)


def _conv_prompt_kernel(a_ref, gate_ref, w_ref, b_ref, lng_ref, lnb_ref,
                        out_ref, tail_ref, uext_scr, ush_scr, y_scr, *, tb, kw, hist):
    i = pl.program_id(0)

    @pl.when(i == 0)
    def _():
        uext_scr[0:hist, :] = jnp.zeros((hist, uext_scr.shape[1]), jnp.float32)

    uext_scr[hist:hist + tb, :] = a_ref[...] * _sigmoid(gate_ref[...])
    span = ush_scr.shape[1]
    for sh in range(1, SUBLANE):
        ush_scr[sh - 1] = uext_scr[sh:sh + span, :]
    ch = a_ref.shape[1]
    rows = 128
    off = hist - (kw - 1)
    nrc = tb // rows
    def lane_chunk(cc, carry):
        cs = pl.ds(pl.multiple_of(cc * LANE, LANE), LANE)
        accs = [jnp.broadcast_to(b_ref[:, cs], (rows, LANE))] * nrc
        for t in range(kw):
            sh = (off + t) % SUBLANE
            wt = w_ref[t:t + 1, cs]
            for rc in range(nrc):
                r0 = rc * rows + (off + t) - sh
                if sh == 0:
                    u = uext_scr[r0:r0 + rows, cs]
                else:
                    u = ush_scr[sh - 1, r0:r0 + rows, cs]
                accs[rc] = accs[rc] + wt * u
        for rc in range(nrc):
            y_scr[rc * rows:(rc + 1) * rows, cs] = accs[rc]
        return carry

    lax.fori_loop(0, ch // LANE, lane_chunk, 0)
    out_ref[...] = _bf16(_layernorm_swish(y_scr[...], lng_ref[...], lnb_ref[...]))
    tail = uext_scr[tb:tb + hist, :]
    uext_scr[0:hist, :] = tail
    tail_ref[...] = tail


def _conv_prompt(ag, w, b, lng, lnb, *, tb):
    s = ag.shape[0]
    ch = ag.shape[1] // 2
    kw = w.shape[0]
    hist = 32
    assert kw - 1 <= hist and s % tb == 0
    kern = functools.partial(_conv_prompt_kernel, tb=tb, kw=kw, hist=hist)
    return pl.pallas_call(
        kern,
        grid=(s // tb,),
        in_specs=[
            pl.BlockSpec((tb, ch), lambda i: (i, 0)),
            pl.BlockSpec((tb, ch), lambda i: (i, 1)),
            pl.BlockSpec((kw, ch), lambda i: (0, 0)),
            pl.BlockSpec((1, ch), lambda i: (0, 0)),
            pl.BlockSpec((1, ch), lambda i: (0, 0)),
            pl.BlockSpec((1, ch), lambda i: (0, 0)),
        ],
        out_specs=[
            pl.BlockSpec((tb, ch), lambda i: (i, 0)),
            pl.BlockSpec((hist, ch), lambda i: (0, 0)),
        ],
        out_shape=[
            jax.ShapeDtypeStruct((s, ch), jnp.bfloat16),
            jax.ShapeDtypeStruct((hist, ch), jnp.float32),
        ],
        scratch_shapes=[pltpu.VMEM((hist + tb, ch), jnp.float32),
                        pltpu.VMEM((SUBLANE - 1, hist + tb - SUBLANE, ch), jnp.float32),
                        pltpu.VMEM((tb, ch), jnp.float32)],
        compiler_params=_params(("arbitrary",)),
        name="conv_prompt",
    )(ag, ag, w, b, lng, lnb)


def _conv_sample_kernel(a_ref, gate_ref, st_ref, w_ref, b_ref, lng_ref, lnb_ref,
                        out_ref, u_ref, *, kw):
    u = a_ref[...] * _sigmoid(gate_ref[...])
    u_ref[...] = u
    acc = b_ref[...] + w_ref[kw - 1:kw, :] * u
    for t in range(kw - 1):
        acc = acc + w_ref[t:t + 1, :] * st_ref[t]
    out_ref[...] = _bf16(_layernorm_swish(acc, lng_ref[...], lnb_ref[...]))


def _conv_sample(ag, state_t, w, b, lng, lnb):
    nb = ag.shape[0]
    ch = ag.shape[1] // 2
    kw = w.shape[0]
    kern = functools.partial(_conv_sample_kernel, kw=kw)
    return pl.pallas_call(
        kern,
        grid=(1,),
        in_specs=[
            pl.BlockSpec((nb, ch), lambda i: (0, 0)),
            pl.BlockSpec((nb, ch), lambda i: (0, 1)),
            pl.BlockSpec((kw - 1, nb, ch), lambda i: (0, 0, 0)),
            pl.BlockSpec((kw, ch), lambda i: (0, 0)),
            pl.BlockSpec((1, ch), lambda i: (0, 0)),
            pl.BlockSpec((1, ch), lambda i: (0, 0)),
            pl.BlockSpec((1, ch), lambda i: (0, 0)),
        ],
        out_specs=[
            pl.BlockSpec((nb, ch), lambda i: (0, 0)),
            pl.BlockSpec((nb, ch), lambda i: (0, 0)),
        ],
        out_shape=[
            jax.ShapeDtypeStruct((nb, ch), jnp.bfloat16),
            jax.ShapeDtypeStruct((nb, ch), jnp.float32),
        ],
        compiler_params=_params(("arbitrary",)),
        name="conv_sample",
    )(ag, ag, state_t, w, b, lng, lnb)


def _flash_kernel(qat_ref, ka_ref, vat_ref, o_ref, s_scr, acc_scr, *, tk):
    i = pl.program_id(1)
    tq = 2 * tk
    qat = jnp.concatenate([qat_ref[0], qat_ref[1]], axis=1)
    acc_scr[...] = jnp.zeros_like(acc_scr)

    def block_of(k):
        return jnp.where(k < 2, 2 * i + k, k - 2)

    def scores(k, slot, diag=None):
        j = k - 2 if diag is None else 2 * i + diag
        r0 = pl.multiple_of(j * tk, tk)
        st = _dot(ka_ref[pl.ds(r0, tk), :], qat)
        if diag is not None:
            key = lax.broadcasted_iota(jnp.int32, (tk, tq), 0) + diag * tk
            qry = lax.broadcasted_iota(jnp.int32, (tk, tq), 1)
            st = jnp.where(key <= qry, st, NEG)
        s_scr[slot] = st
        return jnp.max(st, axis=0, keepdims=True)

    def accumulate(k, slot, m_old, mx):
        m_new = jnp.maximum(m_old, mx)
        alpha = jnp.exp2(m_old - m_new)
        pt = _bf16(jnp.exp2(s_scr[slot] - m_new))
        acc_scr[...] = alpha * acc_scr[...] + _dot(vat_ref[block_of(k)], pt)
        return m_new

    def body(p, carry):
        m, mx0 = carry
        k = 2 * p
        mx1 = scores(k + 1, 1)
        m = accumulate(k, 0, m, mx0)
        mx0 = scores(k + 2, 0)
        m = accumulate(k + 1, 1, m, mx1)
        return m, mx0

    m0 = jnp.full((1, tq), NEG, jnp.float32)
    last = 2 * i + 1

    @pl.when(i == 0)
    def _():
        mx0 = scores(0, 0, diag=0)
        mx1 = scores(1, 1, diag=1)
        m = accumulate(0, 0, m0, mx0)
        accumulate(1, 1, m, mx1)

    @pl.when(i > 0)
    def _():
        mx0 = scores(0, 0, diag=0)
        mx1 = scores(1, 1, diag=1)
        m = accumulate(0, 0, m0, mx0)
        mx0 = scores(2, 0)
        m = accumulate(1, 1, m, mx1)
        m, mx0 = lax.fori_loop(1, i, body, (m, mx0))
        mx1 = scores(last, 1)
        m = accumulate(last - 1, 0, m, mx0)
        accumulate(last, 1, m, mx1)

    acc = acc_scr[...]
    out_t = acc[0:HEAD_DIM, :] / acc[HEAD_DIM:HEAD_DIM + 1, :]
    o_ref[...] = _bf16(jnp.transpose(out_t))


def _flash_prompt(qat, ka, vat):
    s = ka.shape[0]
    nblk, _, tk = qat.shape
    tq = 2 * tk
    assert nblk * tk == s and s % tq == 0 and vat.shape == (nblk, N_HEADS * V_ROWS, tk)
    kern = functools.partial(_flash_kernel, tk=tk)
    return pl.pallas_call(
        kern,
        grid=(N_HEADS, s // tq),
        in_specs=[
            pl.BlockSpec((2, QK_DEPTH, tk), lambda h, i: (i, h, 0)),
            pl.BlockSpec((s, QK_DEPTH), lambda h, i: (0, h)),
            pl.BlockSpec((nblk, V_ROWS, tk), lambda h, i: (0, h, 0)),
        ],
        out_specs=pl.BlockSpec((tq, HEAD_DIM), lambda h, i: (i, h)),
        out_shape=jax.ShapeDtypeStruct((s, ATTN_WIDTH), jnp.bfloat16),
        scratch_shapes=[pltpu.VMEM((2, tk, tq), jnp.float32),
                        pltpu.VMEM((V_ROWS, tq), jnp.float32)],
        compiler_params=_params(("parallel", "arbitrary")),
        name="flash_prompt",
    )(qat, ka, vat)


def _decode_kernel(pt_ref, q_ref, kn_ref, vn_ref, lfn_ref, *refs, npp):
    k_refs = refs[0:npp]
    v_refs = refs[npp:2 * npp]
    lf_refs = refs[2 * npp:3 * npp]
    o_ref = refs[3 * npp]
    m_scr, l_scr, acc_scr, run_scr, lf_scr = refs[3 * npp + 1:]
    g = pl.program_id(1)
    rows = q_ref.shape[0]
    width = k_refs[0].shape[0]
    nch = width // LANE
    q = q_ref[...]

    @pl.when(g == 0)
    def _():
        kn = _bf16(kn_ref[...]).astype(jnp.float32)
        s_new = jnp.sum(q.astype(jnp.float32) * kn, axis=-1, keepdims=True)
        m_scr[...] = s_new
        l_scr[...] = jnp.ones_like(l_scr)
        acc_scr[...] = _bf16(vn_ref[...]).astype(jnp.float32)
        run_scr[...] = lfn_ref[...]

    for p in range(npp):
        lf_scr[p:p + 1, :] = lf_refs[p][...]
    x = lf_scr[...]
    xs = jnp.concatenate([x[:, c * LANE:(c + 1) * LANE] for c in range(nch)], axis=0)
    lane = lax.broadcasted_iota(jnp.int32, xs.shape, 1)
    y = xs
    tot = xs
    for sh in (8, 16, 32, 64):
        y = y + jnp.where(lane + sh < LANE, pltpu.roll(y, LANE - sh, axis=1), 0.0)
        tot = tot + pltpu.roll(tot, sh, axis=1)
    excl = y - xs
    later = jnp.zeros((npp, LANE), jnp.float32)
    pieces = [None] * nch
    for c in reversed(range(nch)):
        pieces[c] = excl[c * npp:(c + 1) * npp, :] + later
        later = later + tot[c * npp:(c + 1) * npp, :]
    within = jnp.concatenate(pieces, axis=1)
    page_tot = later

    hrow = lax.broadcasted_iota(jnp.int32, (rows, width), 0)
    hcol = lax.broadcasted_iota(jnp.int32, (rows, width), 1) & (N_HEADS - 1)
    own = hrow == hcol
    run = run_scr[...]
    scores = [None] * npp
    for p in reversed(range(npp)):
        bias = within[p:p + 1, :] + jnp.concatenate([run] * nch, axis=1)
        s = lax.dot_general(q, _bf16(k_refs[p][...]), _NT, preferred_element_type=jnp.float32)
        scores[p] = jnp.where(own, s + bias, NEG)
        run = run + page_tot[p:p + 1, :]
    run_scr[...] = run
    smax = scores[0]
    for p in range(1, npp):
        smax = jnp.maximum(smax, scores[p])
    m_old = m_scr[...]
    m_new = jnp.maximum(m_old, jnp.max(smax, axis=-1, keepdims=True))
    alpha = jnp.exp(m_old - m_new)
    psum = jnp.zeros((rows, width), jnp.float32)
    acc = alpha * acc_scr[...]
    for p in range(npp):
        pr = jnp.exp(scores[p] - m_new)
        psum = psum + pr
        acc = acc + _dot(_bf16(pr), _bf16(v_refs[p][...]))
    l_new = alpha * l_scr[...] + jnp.sum(psum, axis=-1, keepdims=True)
    m_scr[...] = m_new
    l_scr[...] = l_new
    acc_scr[...] = acc

    @pl.when(g == pl.num_programs(1) - 1)
    def _():
        o_ref[...] = acc / l_new


def _decode_attention(page_table, q16, k_new16, v_new16, lf_row, kc, vc, lfc):
    nb, n_pages = page_table.shape
    npp = PAGES_PER_STEP
    assert n_pages % npp == 0
    ng = n_pages // npp
    width = kc.shape[1]
    rows = q16.shape[1]

    def page_map(p):
        return lambda b, g, pt: (pt[b, (ng - 1 - g) * npp + p], 0, 0)

    per_b = lambda b, g, pt: (b, 0, 0)
    in_specs = [pl.BlockSpec((None, rows, HEAD_DIM), per_b)] * 3
    in_specs.append(pl.BlockSpec((None, 1, LANE), per_b))
    in_specs += [pl.BlockSpec((None, width, HEAD_DIM), page_map(p % npp))
                 for p in range(2 * npp)]
    in_specs += [pl.BlockSpec((None, 1, width), page_map(p)) for p in range(npp)]
    kern = functools.partial(_decode_kernel, npp=npp)
    grid_spec = pltpu.PrefetchScalarGridSpec(
        num_scalar_prefetch=1,
        grid=(nb, ng),
        in_specs=in_specs,
        out_specs=pl.BlockSpec((None, rows, HEAD_DIM), per_b),
        scratch_shapes=[pltpu.VMEM((rows, 1), jnp.float32),
                        pltpu.VMEM((rows, 1), jnp.float32),
                        pltpu.VMEM((rows, HEAD_DIM), jnp.float32),
                        pltpu.VMEM((1, LANE), jnp.float32),
                        pltpu.VMEM((npp, width), jnp.float32)],
    )
    return pl.pallas_call(
        kern,
        grid_spec=grid_spec,
        out_shape=jax.ShapeDtypeStruct((nb, rows, HEAD_DIM), jnp.float32),
        compiler_params=_params(("arbitrary", "arbitrary")),
        name="decode_attention",
    )(page_table, q16, k_new16, v_new16, lf_row, *([kc] * npp), *([vc] * npp), *([lfc] * npp))


def _outproj_kernel(c_ref, a_ref, x_ref, w1_ref, w2_ref, g_ref, h_ref, hn_ref):
    h = x_ref[...] + _dot(c_ref[...], w1_ref[...]) + _dot(a_ref[...], w2_ref[...])
    h_ref[...] = h
    hn_ref[...] = _bf16(_rms(h, g_ref[...]))


def _outproj(conv_out, attn_out, x, w_out, g, *, tm):
    m, d = x.shape
    half = conv_out.shape[1]
    return pl.pallas_call(
        _outproj_kernel,
        grid=(m // tm,),
        in_specs=[
            pl.BlockSpec((tm, half), lambda i: (i, 0)),
            pl.BlockSpec((tm, half), lambda i: (i, 0)),
            pl.BlockSpec((tm, d), lambda i: (i, 0)),
            pl.BlockSpec((half, d), lambda i: (0, 0)),
            pl.BlockSpec((half, d), lambda i: (1, 0)),
            pl.BlockSpec((1, d), lambda i: (0, 0)),
        ],
        out_specs=[
            pl.BlockSpec((tm, d), lambda i: (i, 0)),
            pl.BlockSpec((tm, d), lambda i: (i, 0)),
        ],
        out_shape=[
            jax.ShapeDtypeStruct((m, d), jnp.float32),
            jax.ShapeDtypeStruct((m, d), jnp.bfloat16),
        ],
        compiler_params=_params(("parallel",)),
        name="outproj",
    )(conv_out, attn_out, x, w_out, w_out, g)


def _ffn_prompt_kernel(hn_ref, h_ref, wa_ref, wb_ref, wd_ref, cwa_ref, cwb_ref,
                       cba_ref, cbb_ref, gf_ref,
                       y_ref, ta_ref, tb_ref, acc_scr, up_scr, tail_scr, *, tm, tf):
    i = pl.program_id(0)
    j = pl.program_id(1)
    nj = pl.num_programs(1)
    hist = SUBLANE
    slot = pl.multiple_of(j * hist, hist)

    @pl.when(j == 0)
    def _():
        acc_scr[...] = jnp.zeros_like(acc_scr)

    @pl.when(i == 0)
    def _():
        up_scr[0:hist, :] = jnp.zeros((hist, 2 * tf), jnp.float32)

    @pl.when(i > 0)
    def _():
        up_scr[0:hist, :] = tail_scr[pl.ds(slot, hist), :]

    hn = hn_ref[...]
    up_scr[hist:hist + tm, 0:tf] = _dot(hn, wa_ref[...])
    up_scr[hist:hist + tm, tf:2 * tf] = _dot(hn, wb_ref[...])

    def conv(cols, cw_ref, cb_ref):
        return (cb_ref[...]
                + cw_ref[0:1, :] * up_scr[hist - 2:hist - 2 + tm, cols]
                + cw_ref[1:2, :] * up_scr[hist - 1:hist - 1 + tm, cols]
                + cw_ref[2:3, :] * up_scr[hist:hist + tm, cols])

    ca = conv(slice(0, tf), cwa_ref, cba_ref)
    cb = conv(slice(tf, 2 * tf), cwb_ref, cbb_ref)
    act = _bf16(ca * _sigmoid(ca) * cb)
    acc_scr[...] += _dot(act, wd_ref[...])

    tail = up_scr[tm:tm + hist, :]
    tail_scr[pl.ds(slot, hist), :] = tail

    @pl.when(i == pl.num_programs(0) - 1)
    def _():
        cols = pl.ds(pl.multiple_of(j * tf, tf), tf)
        ta_ref[:, cols] = tail[:, 0:tf]
        tb_ref[:, cols] = tail[:, tf:2 * tf]

    @pl.when(j == nj - 1)
    def _():
        y_ref[...] = _rms(h_ref[...] + acc_scr[...], gf_ref[...])


def _ffn_prompt(hn, h, w_up, w_down, cw, cb, gf, *, tm, tf):
    m, d = h.shape
    dff = w_down.shape[0]
    assert dff % tf == 0 and m % tm == 0
    nj = dff // tf
    kern = functools.partial(_ffn_prompt_kernel, tm=tm, tf=tf)
    return pl.pallas_call(
        kern,
        grid=(m // tm, nj),
        in_specs=[
            pl.BlockSpec((tm, d), lambda i, j: (i, 0)),
            pl.BlockSpec((tm, d), lambda i, j: (i, 0)),
            pl.BlockSpec((d, tf), lambda i, j: (0, j)),
            pl.BlockSpec((d, tf), lambda i, j: (0, j + nj)),
            pl.BlockSpec((tf, d), lambda i, j: (j, 0)),
            pl.BlockSpec((3, tf), lambda i, j: (0, j)),
            pl.BlockSpec((3, tf), lambda i, j: (0, j + nj)),
            pl.BlockSpec((1, tf), lambda i, j: (0, j)),
            pl.BlockSpec((1, tf), lambda i, j: (0, j + nj)),
            pl.BlockSpec((1, d), lambda i, j: (0, 0)),
        ],
        out_specs=[
            pl.BlockSpec((tm, d), lambda i, j: (i, 0)),
            pl.BlockSpec((SUBLANE, dff), lambda i, j: (0, 0)),
            pl.BlockSpec((SUBLANE, dff), lambda i, j: (0, 0)),
        ],
        out_shape=[
            jax.ShapeDtypeStruct((m, d), jnp.float32),
            jax.ShapeDtypeStruct((SUBLANE, dff), jnp.float32),
            jax.ShapeDtypeStruct((SUBLANE, dff), jnp.float32),
        ],
        scratch_shapes=[pltpu.VMEM((tm, d), jnp.float32),
                        pltpu.VMEM((SUBLANE + tm, 2 * tf), jnp.float32),
                        pltpu.VMEM((nj * SUBLANE, 2 * tf), jnp.float32)],
        compiler_params=_params(("arbitrary", "arbitrary")),
        name="ffn_prompt",
    )(hn, h, w_up, w_up, w_down, cw, cw, cb, cb, gf)


def _ffn_sample_kernel(hn_ref, h_ref, wa_ref, wb_ref, wd_ref, sta_ref, stb_ref,
                       cwa_ref, cwb_ref, cba_ref, cbb_ref, gf_ref,
                       y_ref, ua_ref, ub_ref, acc_scr):
    j = pl.program_id(0)

    @pl.when(j == 0)
    def _():
        acc_scr[...] = jnp.zeros_like(acc_scr)

    hn = hn_ref[...]
    ua = _dot(hn, wa_ref[...])
    ub = _dot(hn, wb_ref[...])
    ua_ref[...] = ua
    ub_ref[...] = ub
    ca = cba_ref[...] + cwa_ref[0:1, :] * sta_ref[0] + cwa_ref[1:2, :] * sta_ref[1] + cwa_ref[2:3, :] * ua
    cb = cbb_ref[...] + cwb_ref[0:1, :] * stb_ref[0] + cwb_ref[1:2, :] * stb_ref[1] + cwb_ref[2:3, :] * ub
    act = _bf16(ca * _sigmoid(ca) * cb)
    acc_scr[...] += _dot(act, wd_ref[...])

    @pl.when(j == pl.num_programs(0) - 1)
    def _():
        y_ref[...] = _rms(h_ref[...] + acc_scr[...], gf_ref[...])


def _ffn_sample(hn, h, w_up, w_down, state_t, cw, cb, gf, *, tf):
    m, d = h.shape
    dff = w_down.shape[0]
    nj = dff // tf
    return pl.pallas_call(
        _ffn_sample_kernel,
        grid=(nj,),
        in_specs=[
            pl.BlockSpec((m, d), lambda j: (0, 0)),
            pl.BlockSpec((m, d), lambda j: (0, 0)),
            pl.BlockSpec((d, tf), lambda j: (0, j)),
            pl.BlockSpec((d, tf), lambda j: (0, j + nj)),
            pl.BlockSpec((tf, d), lambda j: (j, 0)),
            pl.BlockSpec((2, m, tf), lambda j: (0, 0, j)),
            pl.BlockSpec((2, m, tf), lambda j: (0, 0, j + nj)),
            pl.BlockSpec((3, tf), lambda j: (0, j)),
            pl.BlockSpec((3, tf), lambda j: (0, j + nj)),
            pl.BlockSpec((1, tf), lambda j: (0, j)),
            pl.BlockSpec((1, tf), lambda j: (0, j + nj)),
            pl.BlockSpec((1, d), lambda j: (0, 0)),
        ],
        out_specs=[
            pl.BlockSpec((m, d), lambda j: (0, 0)),
            pl.BlockSpec((m, tf), lambda j: (0, j)),
            pl.BlockSpec((m, tf), lambda j: (0, j)),
        ],
        out_shape=[
            jax.ShapeDtypeStruct((m, d), jnp.float32),
            jax.ShapeDtypeStruct((m, dff), jnp.float32),
            jax.ShapeDtypeStruct((m, dff), jnp.float32),
        ],
        scratch_shapes=[pltpu.VMEM((m, d), jnp.float32)],
        compiler_params=_params(("arbitrary",)),
        name="ffn_sample",
    )(hn, h, w_up, w_up, w_down, state_t, state_t, cw, cw, cb, cb, gf)


def _pick(n, prefs):
    for p in prefs:
        if n % p == 0:
            return p
    return n


def kernel(x_prompt, x_sample, cache_k, cache_v, cache_logf, state_conv, state_ffn, page_table,
           norm_mix_g, w_in, b_f, conv_dw_w, conv_dw_b, conv_ln_g, conv_ln_b, w_out,
           norm_ffn_g, w_up, ffn_dw_w, ffn_dw_b, w_down, norm_final_g):
    f32 = jnp.float32
    depth = w_in.shape[0]
    assert depth == 1 and x_prompt.shape[0] == 1 and x_sample.shape[1] == 1
    s, d = x_prompt.shape[1], x_prompt.shape[2]
    nb = x_sample.shape[0]
    n_pool, page = cache_k.shape[1], cache_k.shape[2]
    assert cache_k.shape[3:] == (N_HEADS, HEAD_DIM)
    dff = w_down.shape[1]
    conv_ch = conv_dw_w.shape[2]
    assert conv_ch == ATTN_WIDTH
    n_main = 2 * conv_ch + 3 * ATTN_WIDTH

    w_in_bf = _bf16(w_in[0])
    w_main = w_in_bf[:, :n_main]
    q0, v0 = 2 * conv_ch, 2 * conv_ch + 2 * ATTN_WIDTH
    wt_qv = jnp.transpose(jnp.concatenate(
        [w_main[:, q0:q0 + ATTN_WIDTH], w_main[:, v0:v0 + ATTN_WIDTH]], axis=1))
    wf_pad = jnp.pad(w_in_bf[:, n_main:], ((0, 0), (0, LANE - N_HEADS)))
    bf_row = jnp.pad(b_f[0].reshape(1, N_HEADS), ((0, 0), (0, LANE - N_HEADS)))
    w_out_bf = _bf16(w_out[0])
    w_down_bf = _bf16(w_down[0])
    w_up_bf = _bf16(w_up[0])
    tf = _pick(dff, (512, 256))
    row = lambda v: v.reshape(1, -1)
    g_mix, g_ffn, g_fin = row(norm_mix_g[0]), row(norm_ffn_g[0]), row(norm_final_g)
    cw, cb_, lng, lnb = conv_dw_w[0], row(conv_dw_b[0]), row(conv_ln_g[0]), row(conv_ln_b[0])
    fcw, fcb = ffn_dw_w[0], row(ffn_dw_b[0])
    kw = cw.shape[0]
    tril = _bf16(jnp.tril(jnp.ones((SCAN_CHUNK, SCAN_CHUNK), f32)))
    src = jnp.arange(3 * LANE)
    dst = jnp.arange(ATTN_WIDTH)
    psel = _bf16(((src[:, None] % LANE == dst[None, :] // LANE)
                  & (src[:, None] // LANE == dst[None, :] % LANE)
                  & (src[:, None] % LANE < N_HEADS)).astype(f32))

    xp = x_prompt[0]
    ag_p, k_p, v_p, qat_p, ka_p, vat_p, lf_p = _inproj_prompt(
        xp, g_mix, w_main, wt_qv, wf_pad, bf_row, tril, psel, tm=_pick(s, (512, 256)), tn=512)
    conv_p, utail_p = _conv_prompt(ag_p, cw, cb_, lng, lnb, tb=_pick(s, (256, 128)))
    attn_p = _flash_prompt(qat_p, ka_p, vat_p)
    h_p, hn_p = _outproj(conv_p, attn_p, xp, w_out_bf, g_ffn, tm=_pick(s, (512, 256)))
    y_p, ta_p, tb_p = _ffn_prompt(hn_p, h_p, w_up_bf, w_down_bf, fcw, fcb, g_fin,
                                  tm=_pick(s, (512, 256)), tf=tf)

    y_prompt = y_p[None]
    k_prompt = k_p[None, None]
    v_prompt = v_p[None, None]
    logf_prompt = lf_p[:, :N_HEADS].reshape(1, 1, s, N_HEADS)
    conv_prompt = utail_p[utail_p.shape[0] - (kw - 1):][None, None]
    ffn_prompt = jnp.concatenate([ta_p[SUBLANE - 2:], tb_p[SUBLANE - 2:]], axis=1)[None, None]

    xs = x_sample[:, 0, :]
    ag_s, kv_s, q_s, lf_s128 = _inproj_sample(xs, g_mix, w_main, wf_pad, bf_row)
    conv_s, u_s = _conv_sample(ag_s, jnp.swapaxes(state_conv[0], 0, 1), cw, cb_, lng, lnb)

    pad_heads = lambda a: jnp.pad(a.reshape(nb, N_HEADS, HEAD_DIM),
                                  ((0, 0), (0, HEAD_ROWS - N_HEADS), (0, 0)))
    q16 = pad_heads(q_s)
    kn16 = pad_heads(kv_s[:, :ATTN_WIDTH])
    vn16 = pad_heads(kv_s[:, ATTN_WIDTH:])
    lf_s = lf_s128[:, :N_HEADS]
    lf_row = jnp.tile(lf_s, (1, LANE // N_HEADS)).reshape(nb, 1, LANE)
    kc = cache_k[0].reshape(n_pool, page * N_HEADS, HEAD_DIM)
    vc = cache_v[0].reshape(n_pool, page * N_HEADS, HEAD_DIM)
    lfc = cache_logf[0].astype(f32).reshape(n_pool, 1, page * N_HEADS)
    attn_s16 = _decode_attention(page_table, q16, kn16, vn16, lf_row, kc, vc, lfc)
    attn_s = _bf16(attn_s16[:, :N_HEADS, :].reshape(nb, ATTN_WIDTH))

    h_s, hn_s = _outproj(conv_s, attn_s, xs, w_out_bf, g_ffn, tm=nb)
    y_s, ua_s, ub_s = _ffn_sample(hn_s, h_s, w_up_bf, w_down_bf,
                                  jnp.swapaxes(state_ffn[0], 0, 1), fcw, fcb, g_fin, tf=tf)

    y_sample = y_s[:, None, :]
    k_sample = kv_s[:, :ATTN_WIDTH].reshape(1, nb, 1, N_HEADS, HEAD_DIM)
    v_sample = kv_s[:, ATTN_WIDTH:].reshape(1, nb, 1, N_HEADS, HEAD_DIM)
    logf_sample = lf_s.reshape(1, nb, 1, N_HEADS)
    conv_sample = jnp.concatenate([state_conv[0][:, 1:, :], u_s[:, None, :]], axis=1)[None]
    up_s = jnp.concatenate([ua_s, ub_s], axis=1)
    ffn_sample = jnp.concatenate([state_ffn[0][:, 1:, :], up_s[:, None, :]], axis=1)[None]

    return (y_prompt, y_sample, k_prompt, v_prompt, logf_prompt, conv_prompt, ffn_prompt,
            k_sample, v_sample, logf_sample, conv_sample, ffn_sample)
```
